```python
import math
import jax, jax.numpy as jnp
from jax import lax
import numpy as np

D_MODEL = 1024
BATCH = 8
SEQ = 8192
DEPTH = 1
DEC_BATCH = 32
DEC_SEQ = 2048
PAST_LEN = 128

N_MEM = 256
EPS = 1e-6
HG_HEADS = 4
HG_DK = D_MODEL // 8
HG_DV = D_MODEL // 8
HG_WIDTH = HG_HEADS * HG_DV
HG_CHUNK = 64
DA_HEADS = 4
DA_DK = D_MODEL // 16
DA_DV = 2 * DA_DK
DA_WIDTH = DA_HEADS * DA_DV
Q_BLOCK = 128
MIX_WIDTH = HG_WIDTH + DA_WIDTH
IN_COLS = 3 * HG_HEADS * HG_DK + 2 * HG_HEADS * HG_DV + 2 * DA_HEADS * 2 * DA_DK + DA_HEADS * DA_DV
CA_HEADS = 4
CA_DH = D_MODEL // CA_HEADS
N_GROUPS = 4
EXP_PER_GROUP = 8
N_EXPERTS = N_GROUPS * EXP_PER_GROUP
TOP_K = 2
D_EXPERT = D_MODEL // 2
MOE_BLOCK = 128

kernel_name = 'hymba_hgrn2_diffattn_hmoe_encoder'


def rms_norm(x, g):
    x32 = x.astype(jnp.float32)
    y = x32 * lax.rsqrt(jnp.mean(x32 * x32, axis=-1, keepdims=True) + EPS)
    return (y * g.astype(jnp.float32)).astype(x.dtype)


def head_rms(o, g):
    o32 = o.astype(jnp.float32)
    return o32 * lax.rsqrt(jnp.mean(o32 * o32, axis=-1, keepdims=True) + EPS) * g.astype(jnp.float32)


def alibi_slopes(n_heads):
    return jnp.power(2.0, -8.0 * jnp.arange(1, n_heads + 1, dtype=jnp.float32) / n_heads)


def hgrn2_chunk_scan(q, k, v, log_f):
    B, S, H, DK = q.shape
    DV = v.shape[-1]
    n = S // HG_CHUNK

    def to_chunks(t):
        return jnp.moveaxis(t.reshape(B, n, HG_CHUNK, H, t.shape[-1]), 1, 0).transpose(0, 1, 3, 2, 4)

    qc, kc, vc, fc = to_chunks(q), to_chunks(k), to_chunks(v), to_chunks(log_f)
    lower = jnp.tril(jnp.ones((HG_CHUNK, HG_CHUNK), dtype=bool))[:, :, None]

    def step(state, inp):
        qb, kb, vb, fb = inp
        b = jnp.cumsum(fb, axis=2)
        o_inter = jnp.einsum('bhtd,bhdv->bhtv', qb * jnp.exp(b), state)
        diff = b[:, :, :, None, :] - b[:, :, None, :, :]
        decay = jnp.exp(jnp.where(lower, diff, -jnp.inf))
        scores = jnp.einsum('bhtd,bhsd,bhtsd->bhts', qb, kb, decay)
        o_intra = jnp.einsum('bhts,bhsv->bhtv', scores, vb)
        b_last = b[:, :, -1:, :]
        state = jnp.exp(b_last[:, :, 0, :])[..., None] * state + jnp.einsum('bhsd,bhsv->bhdv', kb * jnp.exp(b_last - b), vb)
        return state, o_inter + o_intra

    s0 = jnp.zeros((B, H, DK, DV), jnp.float32)
    _, o = lax.scan(step, s0, (qc, kc, vc, fc))
    return o.transpose(1, 0, 3, 2, 4).reshape(B, S, H, DV)


def diff_attention(q, k, v, lam):
    B, S, H = q.shape[:3]
    nb = S // Q_BLOCK
    scale = DA_DK ** -0.5
    slopes = alibi_slopes(H)
    qb = jnp.moveaxis(q.transpose(0, 2, 3, 1, 4).reshape(B, H, 2, nb, Q_BLOCK, DA_DK), 3, 0)
    kt = k.transpose(0, 2, 3, 1, 4)
    vt = v.transpose(0, 2, 1, 3)
    pos_k = jnp.arange(S)

    def block(inp):
        qblk, idx = inp
        pos_q = idx * Q_BLOCK + jnp.arange(Q_BLOCK)
        dist = jnp.abs(pos_q[:, None] - pos_k[None, :]).astype(jnp.float32)
        bias = -slopes[:, None, None] * dist
        s = jnp.einsum('bhcqd,bhckd->bhcqk', qblk, kt).astype(jnp.float32) * scale + bias[None, :, None]
        p = jax.nn.softmax(s, axis=-1)
        attn = p[:, :, 0] - lam * p[:, :, 1]
        return jnp.einsum('bhqk,bhkv->bhqv', attn.astype(v.dtype), vt)

    o = lax.map(block, (qb, jnp.arange(nb)))
    return jnp.moveaxis(o, 0, 2).reshape(B, H, S, DA_DV).transpose(0, 2, 1, 3)


def memory_cross_attention(h, m, w_cq, w_ckv, w_co):
    B, S, D = h.shape
    q = (h @ w_cq).reshape(B, S, CA_HEADS, CA_DH)
    kv = m @ w_ckv
    k = kv[..., :D].reshape(B, N_MEM, CA_HEADS, CA_DH)
    v = kv[..., D:].reshape(B, N_MEM, CA_HEADS, CA_DH)
    s = jnp.einsum('bqhd,bkhd->bhqk', q, k).astype(jnp.float32) * (CA_DH ** -0.5)
    p = jax.nn.softmax(s, axis=-1).astype(v.dtype)
    o = jnp.einsum('bhqk,bkhd->bqhd', p, v).reshape(B, S, D)
    return o @ w_co


def hier_moe(h, w_rg, b_rg, w_re, b_re, w_gate, w_up, w_down):
    B, S, D = h.shape
    T = B * S
    xt = h.reshape(T, D)
    g_prob = jax.nn.softmax((xt @ w_rg).astype(jnp.float32) + b_rg.astype(jnp.float32), axis=-1)
    g_val, g_idx = lax.top_k(g_prob, 1)
    e_logits = ((xt @ w_re).astype(jnp.float32) + b_re.astype(jnp.float32)).reshape(T, N_GROUPS, EXP_PER_GROUP)
    e_in_group = e_logits[jnp.arange(T), g_idx[:, 0]]
    e_val, e_idx = lax.top_k(e_in_group, TOP_K)
    gate = g_val * jax.nn.softmax(e_val, axis=-1)
    expert = (g_idx * EXP_PER_GROUP + e_idx).reshape(-1)
    weight = gate.reshape(-1)
    token = jnp.repeat(jnp.arange(T, dtype=jnp.int32), TOP_K)
    n_assign = T * TOP_K
    order = jnp.argsort(expert)
    e_sorted, tok_sorted, w_sorted = expert[order], token[order], weight[order]
    counts = jnp.bincount(expert, length=N_EXPERTS)
    starts = jnp.cumsum(counts) - counts
    padded = (counts + MOE_BLOCK - 1) // MOE_BLOCK * MOE_BLOCK
    pad_end = jnp.cumsum(padded)
    pad_start = pad_end - padded
    dest = pad_start[e_sorted] + jnp.arange(n_assign) - starts[e_sorted]
    n_slots = (n_assign + MOE_BLOCK - 1) // MOE_BLOCK * MOE_BLOCK + N_EXPERTS * MOE_BLOCK
    slot_tok = jnp.full((n_slots,), T, dtype=jnp.int32).at[dest].set(tok_sorted)
    slot_w = jnp.zeros((n_slots,), jnp.float32).at[dest].set(w_sorted)
    nb = n_slots // MOE_BLOCK
    block_expert = jnp.minimum(jnp.searchsorted(pad_end, jnp.arange(nb) * MOE_BLOCK, side='right'), N_EXPERTS - 1)
    x_pad = jnp.concatenate([xt, jnp.zeros((1, D), xt.dtype)], axis=0)
    xs = x_pad[slot_tok].reshape(nb, MOE_BLOCK, D)

    def run(inp):
        xb, e = inp
        return (jax.nn.silu(xb @ w_gate[e]) * (xb @ w_up[e])) @ w_down[e]

    ys = lax.map(run, (xs, block_expert)).reshape(n_slots, D)
    out = jnp.zeros((T + 1, D), jnp.float32).at[slot_tok].add(ys.astype(jnp.float32) * slot_w[:, None])
    return out[:T].reshape(B, S, D).astype(h.dtype)


def encoder_layer(x, mem, l, p):
    B, S, _ = x.shape
    f32 = jnp.float32
    h = rms_norm(x, p['g_mix'][l])
    proj = h @ p['w_in'][l]
    widths = [HG_HEADS * HG_DK] * 3 + [HG_HEADS * HG_DV] * 2 + [DA_HEADS * 2 * DA_DK] * 2 + [DA_HEADS * DA_DV]
    points = np.cumsum(widths)[:-1].tolist()
    hq, hf_fwd, hf_bwd, hi, hg, dq, dk, dv = jnp.split(proj, points, axis=-1)

    lb = jnp.cumsum(jax.nn.softmax(p['hg_lb'].astype(f32), axis=0), axis=0)[l]

    def gates(fx):
        f = lb + (1.0 - lb) * jax.nn.sigmoid(fx.astype(f32))
        return jnp.log(f).reshape(B, S, HG_HEADS, HG_DK), (1.0 - f).reshape(B, S, HG_HEADS, HG_DK)

    q_hg = jax.nn.silu(hq.astype(f32)).reshape(B, S, HG_HEADS, HG_DK)
    i_hg = hi.astype(f32).reshape(B, S, HG_HEADS, HG_DV)
    lf_f, k_f = gates(hf_fwd)
    lf_b, k_b = gates(hf_bwd)
    o_fwd = hgrn2_chunk_scan(q_hg, k_f, i_hg, lf_f)
    o_bwd = jnp.flip(hgrn2_chunk_scan(jnp.flip(q_hg, 1), jnp.flip(k_b, 1), jnp.flip(i_hg, 1), jnp.flip(lf_b, 1)), 1)
    o_hg = head_rms(o_fwd + o_bwd, p['g_hg'][l].reshape(HG_HEADS, HG_DV)) * jax.nn.silu(hg.astype(f32)).reshape(B, S, HG_HEADS, HG_DV)

    lam_init = 0.8 - 0.6 * math.exp(-0.3 * l)
    lam = (jnp.exp(jnp.sum(p['lam_q1'][l].astype(f32) * p['lam_k1'][l].astype(f32)))
           - jnp.exp(jnp.sum(p['lam_q2'][l].astype(f32) * p['lam_k2'][l].astype(f32))) + lam_init)
    o_da = diff_attention(dq.reshape(B, S, DA_HEADS, 2, DA_DK), dk.reshape(B, S, DA_HEADS, 2, DA_DK),
                          dv.reshape(B, S, DA_HEADS, DA_DV), lam)
    o_da = head_rms(o_da, p['g_da'][l]) * (1.0 - lam_init)

    mix = jnp.concatenate([o_hg.reshape(B, S, HG_WIDTH), o_da.reshape(B, S, DA_WIDTH)], axis=-1).astype(x.dtype)
    x = x + mix @ p['w_out'][l]

    x = x + memory_cross_attention(rms_norm(x, p['g_ca'][l]), rms_norm(mem, p['g_mem'][l]),
                                   p['w_cq'][l], p['w_ckv'][l], p['w_co'][l])

    x = x + hier_moe(rms_norm(x, p['g_ffn'][l]), p['w_rg'][l], p['b_rg'][l], p['w_re'][l], p['b_re'][l],
                     p['w_gate'][l], p['w_up'][l], p['w_down'][l])
    return x


def encoder_trunk(x, mem, p):
    for l in range(DEPTH):
        x = encoder_layer(x, mem, l, p)
    return rms_norm(x, p['g_final'])


def setup_inputs(seed: int = 0) -> dict:
    key = jax.random.key(seed)
    ks = jax.random.split(key, 32)
    nrm = jax.random.normal
    f32 = jnp.float32

    def gain(k, shape):
        return 1.0 + 0.02 * nrm(k, shape, f32)

    return {
        'x_prompt': nrm(ks[0], (BATCH, SEQ, D_MODEL), f32),
        'x_sample': nrm(ks[1], (DEC_BATCH, DEC_SEQ, D_MODEL), f32),
        'mem_prompt': nrm(ks[2], (BATCH, N_MEM, D_MODEL), f32),
        'mem_sample': nrm(ks[3], (DEC_BATCH, N_MEM, D_MODEL), f32),
        'g_mix': gain(ks[4], (DEPTH, D_MODEL)),
        'w_in': nrm(ks[5], (DEPTH, D_MODEL, IN_COLS), f32) * D_MODEL ** -0.5,
        'hg_lb': 0.1 * nrm(ks[6], (DEPTH + 1, HG_HEADS * HG_DK), f32),
        'g_hg': gain(ks[7], (DEPTH, HG_HEADS * HG_DV)),
        'lam_q1': 0.1 * nrm(ks[8], (DEPTH, DA_DK), f32),
        'lam_k1': 0.1 * nrm(ks[9], (DEPTH, DA_DK), f32),
        'lam_q2': 0.1 * nrm(ks[10], (DEPTH, DA_DK), f32),
        'lam_k2': 0.1 * nrm(ks[11], (DEPTH, DA_DK), f32),
        'g_da': gain(ks[12], (DEPTH, DA_DV)),
        'w_out': nrm(ks[13], (DEPTH, MIX_WIDTH, D_MODEL), f32) * MIX_WIDTH ** -0.5,
        'g_ca': gain(ks[14], (DEPTH, D_MODEL)),
        'g_mem': gain(ks[15], (DEPTH, D_MODEL)),
        'w_cq': nrm(ks[16], (DEPTH, D_MODEL, D_MODEL), f32) * D_MODEL ** -0.5,
        'w_ckv': nrm(ks[17], (DEPTH, D_MODEL, 2 * D_MODEL), f32) * D_MODEL ** -0.5,
        'w_co': nrm(ks[18], (DEPTH, D_MODEL, D_MODEL), f32) * D_MODEL ** -0.5,
        'g_ffn': gain(ks[19], (DEPTH, D_MODEL)),
        'w_rg': nrm(ks[20], (DEPTH, D_MODEL, N_GROUPS), f32) * D_MODEL ** -0.5,
        'b_rg': 0.01 * nrm(ks[21], (DEPTH, N_GROUPS), f32),
        'w_re': nrm(ks[22], (DEPTH, D_MODEL, N_EXPERTS), f32) * D_MODEL ** -0.5,
        'b_re': 0.01 * nrm(ks[23], (DEPTH, N_EXPERTS), f32),
        'w_gate': nrm(ks[24], (DEPTH, N_EXPERTS, D_MODEL, D_EXPERT), f32) * D_MODEL ** -0.5,
        'w_up': nrm(ks[25], (DEPTH, N_EXPERTS, D_MODEL, D_EXPERT), f32) * D_MODEL ** -0.5,
        'w_down': nrm(ks[26], (DEPTH, N_EXPERTS, D_EXPERT, D_MODEL), f32) * D_EXPERT ** -0.5,
        'g_final': gain(ks[27], (D_MODEL,)),
    }


def reference(x_prompt, x_sample, mem_prompt, mem_sample, g_mix, w_in, hg_lb, g_hg, lam_q1, lam_k1, lam_q2, lam_k2,
              g_da, w_out, g_ca, g_mem, w_cq, w_ckv, w_co, g_ffn, w_rg, b_rg, w_re, b_re, w_gate, w_up, w_down, g_final):
    p = {'g_mix': g_mix, 'w_in': w_in, 'hg_lb': hg_lb, 'g_hg': g_hg, 'lam_q1': lam_q1, 'lam_k1': lam_k1,
         'lam_q2': lam_q2, 'lam_k2': lam_k2, 'g_da': g_da, 'w_out': w_out, 'g_ca': g_ca, 'g_mem': g_mem,
         'w_cq': w_cq, 'w_ckv': w_ckv, 'w_co': w_co, 'g_ffn': g_ffn, 'w_rg': w_rg, 'b_rg': b_rg, 'w_re': w_re,
         'b_re': b_re, 'w_gate': w_gate, 'w_up': w_up, 'w_down': w_down, 'g_final': g_final}
    y_prompt = encoder_trunk(x_prompt, mem_prompt, p)
    y_sample = encoder_trunk(x_sample, mem_sample, p)
    return (y_prompt, y_sample)
```

```python
import functools
import math

import jax
import jax.numpy as jnp
from jax import lax
from jax.experimental import pallas as pl
from jax.experimental.pallas import tpu as pltpu

F32 = jnp.float32
BF16 = jnp.bfloat16

D_MODEL = 1024
N_MEM = 256
EPS = 1e-6
HG_HEADS = 4
HG_D = 128
HG_WIDTH = HG_HEADS * HG_D
DA_HEADS = 4
DA_DK = 64
DA_DV = 128
DA_WIDTH = DA_HEADS * DA_DV
IN_COLS = 4096
CA_HEADS = 4
CA_DH = 256
N_GROUPS = 4
EXP_PER_GROUP = 8
N_EXPERTS = 32
D_EXPERT = 512
LANES = 128

CB_HQ, CB_FF, CB_FB, CB_HI = 0, 4, 8, 12
CB_DQ, CB_DK, CB_DV = 20, 24, 28
CB_HGATE_512 = 4

HG_CHUNK = 64
HG_SUB = 16
EXP_CLAMP = 80.0
MOE_BLOCK = 256

VMEM_LIMIT = 48 * 1024 * 1024

PROJ_TM = 512
HG_ROWS = 256
ATT_TQ = 512
ATT_TK = 512
POST_TM = 256


def _cparams(sem, **kw):
    return pltpu.CompilerParams(dimension_semantics=sem, vmem_limit_bytes=VMEM_LIMIT, **kw)


def _rms(x, g):
    return x * lax.rsqrt(jnp.mean(x * x, axis=-1, keepdims=True) + EPS) * g


def _silu(x):
    return x * (1.0 / (1.0 + jnp.exp(-x)))


def _inproj_kernel(x_ref, g_ref, w_ref, o_ref, *, n_chunk):
    h = _rms(x_ref[...], g_ref[...]).astype(BF16)
    for j in range(IN_COLS // n_chunk):
        sl = slice(j * n_chunk, (j + 1) * n_chunk)
        o_ref[:, sl] = jnp.dot(h, w_ref[:, sl], preferred_element_type=F32).astype(BF16)


def _inproj(x2d, g_mix, w_in_bf, tm):
    T = x2d.shape[0]
    return pl.pallas_call(
        functools.partial(_inproj_kernel, n_chunk=512),
        grid=(T // tm,),
        in_specs=[pl.BlockSpec((tm, D_MODEL), lambda i: (i, 0)),
                  pl.BlockSpec((1, D_MODEL), lambda i: (0, 0)),
                  pl.BlockSpec((D_MODEL, IN_COLS), lambda i: (0, 0))],
        out_specs=pl.BlockSpec((tm, IN_COLS), lambda i: (i, 0)),
        out_shape=jax.ShapeDtypeStruct((T, IN_COLS), BF16),
        compiler_params=_cparams(("parallel",)),
        name="inproj",
    )(x2d, g_mix, w_in_bf)


def _hgrn_chunk(q_raw, f_raw, v_raw, lb, st_ref, tri, tri_blk, reverse):
    C, S = HG_CHUNK, HG_SUB
    q = _silu(q_raw.astype(F32))
    f = lb + (1.0 - lb) * (1.0 / (1.0 + jnp.exp(-f_raw.astype(F32))))
    lf = jnp.log(f)
    k = 1.0 - f
    cum = jnp.dot(tri, lf, preferred_element_type=F32, precision=lax.Precision.HIGHEST)
    cum_loc = jnp.dot(tri_blk, lf, preferred_element_type=F32, precision=lax.Precision.HIGHEST)
    ref_pt = cum - cum_loc
    st = st_ref[...]
    q_inter = (q * jnp.exp(cum)).astype(BF16)
    o_inter = lax.dot_general(q_inter, st.astype(BF16), (((1,), (1,)), ((), ())), preferred_element_type=F32)
    q_loc = (q * jnp.exp(cum_loc)).astype(BF16)
    v = v_raw
    n_sub = C // S
    outs = []
    for i in range(n_sub):
        lo, hi = (i * S, C) if reverse else (0, (i + 1) * S)
        ref_i = ref_pt[i * S:i * S + 1, :]
        k_i = (k[lo:hi] * jnp.exp(jnp.minimum(ref_i - cum[lo:hi], EXP_CLAMP))).astype(BF16)
        sc = lax.dot_general(q_loc[i * S:(i + 1) * S], k_i, (((1,), (1,)), ((), ())), preferred_element_type=F32)
        r_g = lax.broadcasted_iota(jnp.int32, (S, hi - lo), 0) + i * S
        c_g = lax.broadcasted_iota(jnp.int32, (S, hi - lo), 1) + lo
        keep = (c_g >= r_g) if reverse else (c_g <= r_g)
        sc = jnp.where(keep, sc, 0.0).astype(BF16)
        outs.append(jnp.dot(sc, v[lo:hi], preferred_element_type=F32))
    o = o_inter + jnp.concatenate(outs, axis=0)
    edge = cum[0:1, :] if reverse else cum[C - 1:C, :]
    k_st = (k * jnp.exp(edge - cum)).astype(BF16)
    st_ref[...] = st * jnp.exp(edge) + lax.dot_general(v, k_st, (((0,), (0,)), ((), ())), preferred_element_type=F32)
    return o


def _hgrn_kernel(lb_ref, qf_ref, ff_ref, vf_ref, qb_ref, fb_ref, vb_ref, of_ref, ob_ref, stf_ref, stb_ref, *, rows):
    @pl.when(pl.program_id(2) == 0)
    def _():
        stf_ref[...] = jnp.zeros_like(stf_ref)
        stb_ref[...] = jnp.zeros_like(stb_ref)

    C, S = HG_CHUNK, HG_SUB
    r = lax.broadcasted_iota(jnp.int32, (C, C), 0)
    c = lax.broadcasted_iota(jnp.int32, (C, C), 1)
    same = (r // S) == (c // S)
    tri_f = (c <= r).astype(F32)
    tri_b = (c >= r).astype(F32)
    blk_f = jnp.where(same, tri_f, 0.0)
    blk_b = jnp.where(same, tri_b, 0.0)
    lb = lb_ref[...]
    n = rows // C

    def body(j, carry):
        a = pl.multiple_of(j * C, C)
        of_ref[pl.ds(a, C), :] = _hgrn_chunk(qf_ref[pl.ds(a, C), :], ff_ref[pl.ds(a, C), :], vf_ref[pl.ds(a, C), :],
                                             lb, stf_ref, tri_f, blk_f, False).astype(of_ref.dtype)
        b = pl.multiple_of((n - 1 - j) * C, C)
        ob_ref[pl.ds(b, C), :] = _hgrn_chunk(qb_ref[pl.ds(b, C), :], fb_ref[pl.ds(b, C), :], vb_ref[pl.ds(b, C), :],
                                             lb, stb_ref, tri_b, blk_b, True).astype(ob_ref.dtype)
        return carry

    lax.fori_loop(0, n, body, 0)


def _hgrn(proj3, lb, rows):
    B, S, _ = proj3.shape
    n = S // rows

    def fwd(cb):
        return pl.BlockSpec((None, rows, LANES), lambda b, h, c: (b, c, cb + h))

    def bwd(cb):
        return pl.BlockSpec((None, rows, LANES), lambda b, h, c: (b, n - 1 - c, cb + h))

    out = jax.ShapeDtypeStruct((B, S, HG_WIDTH), BF16)
    return pl.pallas_call(
        functools.partial(_hgrn_kernel, rows=rows),
        grid=(B, HG_HEADS, n),
        in_specs=[pl.BlockSpec((None, 1, LANES), lambda b, h, c: (h, 0, 0)),
                  fwd(CB_HQ), fwd(CB_FF), fwd(CB_HI), bwd(CB_HQ), bwd(CB_FB), bwd(CB_HI)],
        out_specs=[pl.BlockSpec((None, rows, LANES), lambda b, h, c: (b, c, h)),
                   pl.BlockSpec((None, rows, LANES), lambda b, h, c: (b, n - 1 - c, h))],
        out_shape=[out, out],
        scratch_shapes=[pltpu.VMEM((HG_D, HG_D), F32), pltpu.VMEM((HG_D, HG_D), F32)],
        compiler_params=_cparams(("parallel", "parallel", "arbitrary")),
        name="hgrn2",
    )(lb.reshape(HG_HEADS, 1, HG_D), proj3, proj3, proj3, proj3, proj3, proj3)


def _attn_kernel(sc_ref, q_ref, k_ref, v_ref, g_ref, o_ref, m_ref, l_ref, acc_ref, *, tq, tk, out_scale):
    h, qi, ki = pl.program_id(1), pl.program_id(2), pl.program_id(3)

    @pl.when(ki == 0)
    def _():
        m_ref[...] = jnp.full_like(m_ref, -jnp.inf)
        l_ref[...] = jnp.zeros_like(l_ref)
        acc_ref[...] = jnp.zeros_like(acc_ref)

    q = q_ref[...] * (DA_DK ** -0.5)
    k = k_ref[...]
    v = v_ref[...]
    lane = lax.broadcasted_iota(jnp.int32, (tq, LANES), 1)
    dist = jnp.abs(lax.broadcasted_iota(jnp.int32, (tq, tk), 0) - lax.broadcasted_iota(jnp.int32, (tq, tk), 1)
                   + (qi * tq - ki * tk))
    bias = dist.astype(F32) * (-sc_ref[h])
    for c in range(2):
        qc = jnp.where((lane < DA_DK) == (c == 0), q, jnp.zeros_like(q))
        s = lax.dot_general(qc, k, (((1,), (1,)), ((), ())), preferred_element_type=F32) + bias
        m_old = m_ref[c]
        m_new = jnp.maximum(m_old, jnp.max(s, axis=-1, keepdims=True))
        alpha = jnp.exp(m_old - m_new)
        p = jnp.exp(s - m_new)
        l_ref[c] = alpha * l_ref[c] + jnp.sum(p, axis=-1, keepdims=True)
        acc_ref[c] = alpha * acc_ref[c] + jnp.dot(p.astype(BF16), v, preferred_element_type=F32)
        m_ref[c] = m_new

    @pl.when(ki == pl.num_programs(3) - 1)
    def _():
        lam = sc_ref[DA_HEADS]
        o = acc_ref[0] / l_ref[0] - lam * (acc_ref[1] / l_ref[1])
        o_ref[...] = (_rms(o, g_ref[...]) * out_scale).astype(o_ref.dtype)


def _diff_attn(proj3, scalars, g_da, tq, tk, out_scale):
    B, S, _ = proj3.shape
    return pl.pallas_call(
        functools.partial(_attn_kernel, tq=tq, tk=tk, out_scale=out_scale),
        grid_spec=pltpu.PrefetchScalarGridSpec(
            num_scalar_prefetch=1,
            grid=(B, DA_HEADS, S // tq, S // tk),
            in_specs=[pl.BlockSpec((None, tq, LANES), lambda b, h, i, j, sc: (b, i, CB_DQ + h)),
                      pl.BlockSpec((None, tk, LANES), lambda b, h, i, j, sc: (b, j, CB_DK + h)),
                      pl.BlockSpec((None, tk, LANES), lambda b, h, i, j, sc: (b, j, CB_DV + h)),
                      pl.BlockSpec((1, DA_DV), lambda b, h, i, j, sc: (0, 0))],
            out_specs=pl.BlockSpec((None, tq, LANES), lambda b, h, i, j, sc: (b, i, h)),
            scratch_shapes=[pltpu.VMEM((2, tq, 1), F32), pltpu.VMEM((2, tq, 1), F32),
                            pltpu.VMEM((2, tq, DA_DV), F32)]),
        out_shape=jax.ShapeDtypeStruct((B, S, DA_WIDTH), BF16),
        compiler_params=_cparams(("parallel", "parallel", "parallel", "arbitrary")),
        name="diff_attn",
    )(scalars, proj3, proj3, proj3, g_da)


def _memkv_kernel(m_ref, g_ref, w_ref, o_ref):
    h = _rms(m_ref[...], g_ref[...]).astype(BF16)
    o_ref[...] = jnp.dot(h, w_ref[...], preferred_element_type=F32).astype(BF16)


def _memkv(mem2d, g_mem, w_ckv_bf):
    T = mem2d.shape[0]
    return pl.pallas_call(
        _memkv_kernel,
        grid=(T // N_MEM,),
        in_specs=[pl.BlockSpec((N_MEM, D_MODEL), lambda i: (i, 0)),
                  pl.BlockSpec((1, D_MODEL), lambda i: (0, 0)),
                  pl.BlockSpec((D_MODEL, 2 * D_MODEL), lambda i: (0, 0))],
        out_specs=pl.BlockSpec((N_MEM, 2 * D_MODEL), lambda i: (i, 0)),
        out_shape=jax.ShapeDtypeStruct((T, 2 * D_MODEL), BF16),
        compiler_params=_cparams(("parallel",)),
        name="mem_kv",
    )(mem2d, g_mem, w_ckv_bf)


def _post_kernel(x_ref, of_ref, ob_ref, hgate_ref, oda_ref, kv_ref, ghg_ref, wout_ref, gca_ref, wcq_ref, wco_ref,
                 gffn_ref, wr_ref, br_ref,
                 x2_ref, hn_ref, ri_ref, rw_ref, cnt_ref, carry_ref, *, tm):
    @pl.when(pl.program_id(0) == 0)
    def _():
        carry_ref[...] = jnp.zeros_like(carry_ref)

    o_hg = of_ref[...].astype(F32) + ob_ref[...].astype(F32)
    ghg = ghg_ref[...]
    parts = [_rms(o_hg[:, i * HG_D:(i + 1) * HG_D], ghg[:, i * HG_D:(i + 1) * HG_D]) for i in range(HG_HEADS)]
    mix_hg = (jnp.concatenate(parts, axis=-1) * _silu(hgate_ref[...].astype(F32))).astype(BF16)
    x1 = (x_ref[...] + jnp.dot(mix_hg, wout_ref[:HG_WIDTH, :], preferred_element_type=F32)
          + jnp.dot(oda_ref[...], wout_ref[HG_WIDTH:, :], preferred_element_type=F32))

    hq = jnp.dot(_rms(x1, gca_ref[...]).astype(BF16), wcq_ref[...], preferred_element_type=F32).astype(BF16)
    heads = []
    for i in range(CA_HEADS):
        kh = kv_ref[:, i * CA_DH:(i + 1) * CA_DH]
        vh = kv_ref[:, D_MODEL + i * CA_DH:D_MODEL + (i + 1) * CA_DH]
        s = lax.dot_general(hq[:, i * CA_DH:(i + 1) * CA_DH], kh, (((1,), (1,)), ((), ())),
                            preferred_element_type=F32) * (CA_DH ** -0.5)
        p = jnp.exp(s - jnp.max(s, axis=-1, keepdims=True))
        p = (p / jnp.sum(p, axis=-1, keepdims=True)).astype(BF16)
        heads.append(jnp.dot(p, vh, preferred_element_type=F32).astype(BF16))
    x2 = x1 + jnp.dot(jnp.concatenate(heads, axis=-1), wco_ref[...], preferred_element_type=F32)
    x2_ref[...] = x2

    hn = _rms(x2, gffn_ref[...])
    hn_ref[...] = hn
    logits = jnp.dot(hn, wr_ref[...], preferred_element_type=F32, precision=lax.Precision.HIGHEST) + br_ref[...]
    lane = lax.broadcasted_iota(jnp.int32, (tm, LANES), 1)
    big = jnp.int32(1 << 20)
    neg = jnp.float32(-jnp.inf)

    def first_max(vals):
        m = jnp.max(vals, axis=-1, keepdims=True)
        return m, jnp.min(jnp.where(vals == m, lane, big), axis=-1, keepdims=True)

    gl = jnp.where(lane < N_GROUPS, logits, neg)
    gmax, gidx = first_max(gl)
    g_val = 1.0 / jnp.sum(jnp.exp(gl - gmax), axis=-1, keepdims=True)
    e_lane = lane - N_GROUPS
    in_grp = (e_lane >= 0) & (e_lane < N_EXPERTS) & ((e_lane // EXP_PER_GROUP) == gidx)
    el = jnp.where(in_grp, logits, neg)
    m1, i1 = first_max(el)
    m2, i2 = first_max(jnp.where(lane == i1, neg, el))
    t = jnp.exp(m2 - m1)
    w1 = g_val / (1.0 + t)
    w2 = g_val * t / (1.0 + t)

    hit1, hit2 = lane == i1, lane == i2
    onehot = (hit1 | hit2).astype(BF16)
    r_i = lax.broadcasted_iota(jnp.int32, (tm, tm), 0)
    c_i = lax.broadcasted_iota(jnp.int32, (tm, tm), 1)
    before = jnp.dot((c_i < r_i).astype(BF16), onehot, preferred_element_type=F32) + carry_ref[...]
    r1 = jnp.sum(jnp.where(hit1, before, 0.0), axis=-1, keepdims=True).astype(jnp.int32)
    r2 = jnp.sum(jnp.where(hit2, before, 0.0), axis=-1, keepdims=True).astype(jnp.int32)
    carry_ref[...] += jnp.sum(onehot.astype(F32), axis=0, keepdims=True)
    cnt_ref[...] = carry_ref[...]
    ri_ref[...] = jnp.where(lane == 0, i1 - N_GROUPS, jnp.where(lane == 1, i2 - N_GROUPS,
                            jnp.where(lane == 2, r1, jnp.where(lane == 3, r2, 0))))
    rw_ref[...] = jnp.where(lane == 0, w1, jnp.where(lane == 1, w2, 0.0))


def _post(x2d, o_f, o_b, proj, o_da, kv3, S, w, tm):
    T = x2d.shape[0]
    per_b = S // tm
    row = lambda i: (i, 0)
    const = lambda i: (0, 0)
    return pl.pallas_call(
        functools.partial(_post_kernel, tm=tm),
        grid=(T // tm,),
        in_specs=[pl.BlockSpec((tm, D_MODEL), row),
                  pl.BlockSpec((tm, HG_WIDTH), row),
                  pl.BlockSpec((tm, HG_WIDTH), row),
                  pl.BlockSpec((tm, HG_WIDTH), lambda i: (i, CB_HGATE_512)),
                  pl.BlockSpec((tm, DA_WIDTH), row),
                  pl.BlockSpec((None, N_MEM, 2 * D_MODEL), lambda i: (i // per_b, 0, 0)),
                  pl.BlockSpec((1, HG_WIDTH), const),
                  pl.BlockSpec((D_MODEL, D_MODEL), const),
                  pl.BlockSpec((1, D_MODEL), const),
                  pl.BlockSpec((D_MODEL, D_MODEL), const),
                  pl.BlockSpec((D_MODEL, D_MODEL), const),
                  pl.BlockSpec((1, D_MODEL), const),
                  pl.BlockSpec((D_MODEL, LANES), const),
                  pl.BlockSpec((1, LANES), const)],
        out_specs=[pl.BlockSpec((tm, D_MODEL), row),
                   pl.BlockSpec((tm, D_MODEL), row),
                   pl.BlockSpec((tm, LANES), row),
                   pl.BlockSpec((tm, LANES), row),
                   pl.BlockSpec((1, LANES), const)],
        out_shape=[jax.ShapeDtypeStruct((T, D_MODEL), F32),
                   jax.ShapeDtypeStruct((T, D_MODEL), F32),
                   jax.ShapeDtypeStruct((T, LANES), jnp.int32),
                   jax.ShapeDtypeStruct((T, LANES), F32),
                   jax.ShapeDtypeStruct((1, LANES), F32)],
        scratch_shapes=[pltpu.VMEM((1, LANES), F32)],
        compiler_params=_cparams(("arbitrary",)),
        name="post_mixer",
    )(x2d, o_f, o_b, proj, o_da, kv3, w["g_hg"], w["w_out"], w["g_ca"], w["w_cq"], w["w_co"], w["g_ffn"],
      w["w_r"], w["b_r"])


def _dispatch_kernel(dest_ref, hn_ref, xs_in_ref, xs_ref, sem, *, tm):
    del xs_in_ref
    base = pl.program_id(0) * tm

    def copy(t, kk):
        return pltpu.make_async_copy(hn_ref.at[pl.ds(base + t, 1), :],
                                     xs_ref.at[pl.ds(dest_ref[0, 2 * t + kk], 1), :], sem)

    def start(t, c):
        copy(t, 0).start()
        copy(t, 1).start()
        return c

    def wait(t, c):
        copy(t, 0).wait()
        copy(t, 1).wait()
        return c

    lax.fori_loop(0, tm, start, 0)
    lax.fori_loop(0, tm, wait, 0)


def _dispatch(hn, dest, n_slots, tm):
    T = hn.shape[0]
    xs0 = jnp.zeros((n_slots, D_MODEL), F32)
    return pl.pallas_call(
        functools.partial(_dispatch_kernel, tm=tm),
        grid=(T // tm,),
        in_specs=[pl.BlockSpec((None, 1, 2 * tm), lambda i: (i, 0, 0), memory_space=pltpu.SMEM),
                  pl.BlockSpec(memory_space=pl.ANY),
                  pl.BlockSpec(memory_space=pl.ANY)],
        out_specs=pl.BlockSpec(memory_space=pl.ANY),
        out_shape=jax.ShapeDtypeStruct((n_slots, D_MODEL), F32),
        scratch_shapes=[pltpu.SemaphoreType.DMA(())],
        input_output_aliases={2: 0},
        compiler_params=_cparams(("arbitrary",), has_side_effects=True),
        name="moe_dispatch",
    )(dest.reshape(T // tm, 1, 2 * tm), hn, xs0)


def _expert_kernel(be_ref, nu_ref, xs_ref, wg_ref, wu_ref, wd_ref, ys_ref):
    del be_ref

    @pl.when(pl.program_id(0) < nu_ref[0])
    def _():
        xb = xs_ref[...].astype(BF16)
        a = jnp.dot(xb, wg_ref[...], preferred_element_type=F32)
        u = jnp.dot(xb, wu_ref[...], preferred_element_type=F32)
        hmid = (_silu(a) * u).astype(BF16)
        ys_ref[...] = jnp.dot(hmid, wd_ref[...], preferred_element_type=F32)

    @pl.when(pl.program_id(0) >= nu_ref[0])
    def _():
        ys_ref[...] = jnp.zeros_like(ys_ref)


def _experts(xs, block_expert, n_used, w_gate, w_up, w_down):
    n_slots = xs.shape[0]
    nb = n_slots // MOE_BLOCK
    return pl.pallas_call(
        _expert_kernel,
        grid_spec=pltpu.PrefetchScalarGridSpec(
            num_scalar_prefetch=2,
            grid=(nb,),
            in_specs=[pl.BlockSpec((MOE_BLOCK, D_MODEL), lambda i, be, nu: (i, 0)),
                      pl.BlockSpec((None, D_MODEL, D_EXPERT), lambda i, be, nu: (be[i], 0, 0)),
                      pl.BlockSpec((None, D_MODEL, D_EXPERT), lambda i, be, nu: (be[i], 0, 0)),
                      pl.BlockSpec((None, D_EXPERT, D_MODEL), lambda i, be, nu: (be[i], 0, 0))],
            out_specs=pl.BlockSpec((MOE_BLOCK, D_MODEL), lambda i, be, nu: (i, 0))),
        out_shape=jax.ShapeDtypeStruct((n_slots, D_MODEL), F32),
        compiler_params=_cparams(("arbitrary",)),
        name="moe_experts",
    )(block_expert, n_used, xs, w_gate, w_up, w_down)


def _combine_kernel(dest_ref, x2_ref, rw_ref, g_ref, ys_ref, y_ref, buf_ref, sem, *, tm):
    def copy(t, kk):
        return pltpu.make_async_copy(ys_ref.at[pl.ds(dest_ref[0, 2 * t + kk], 1), :],
                                     buf_ref.at[kk, pl.ds(t, 1), :], sem)

    def start(t, c):
        copy(t, 0).start()
        copy(t, 1).start()
        return c

    def wait(t, c):
        copy(t, 0).wait()
        copy(t, 1).wait()
        return c

    lax.fori_loop(0, tm, start, 0)
    lax.fori_loop(0, tm, wait, 0)
    rw = rw_ref[...]
    lane = lax.broadcasted_iota(jnp.int32, rw.shape, 1)
    w1 = jnp.sum(jnp.where(lane == 0, rw, 0.0), axis=-1, keepdims=True)
    w2 = jnp.sum(jnp.where(lane == 1, rw, 0.0), axis=-1, keepdims=True)
    x3 = x2_ref[...] + (buf_ref[0] * w1 + buf_ref[1] * w2)
    y_ref[...] = _rms(x3, g_ref[...])


def _combine(x2, rw, g_final, ys, dest, tm):
    T = x2.shape[0]
    return pl.pallas_call(
        functools.partial(_combine_kernel, tm=tm),
        grid=(T // tm,),
        in_specs=[pl.BlockSpec((None, 1, 2 * tm), lambda i: (i, 0, 0), memory_space=pltpu.SMEM),
                  pl.BlockSpec((tm, D_MODEL), lambda i: (i, 0)),
                  pl.BlockSpec((tm, LANES), lambda i: (i, 0)),
                  pl.BlockSpec((1, D_MODEL), lambda i: (0, 0)),
                  pl.BlockSpec(memory_space=pl.ANY)],
        out_specs=pl.BlockSpec((tm, D_MODEL), lambda i: (i, 0)),
        out_shape=jax.ShapeDtypeStruct((T, D_MODEL), F32),
        scratch_shapes=[pltpu.VMEM((2, tm, D_MODEL), F32), pltpu.SemaphoreType.DMA(())],
        compiler_params=_cparams(("arbitrary",)),
        name="moe_combine",
    )(dest.reshape(T // tm, 1, 2 * tm), x2, rw, g_final, ys)


def _tile(n, pref):
    return min(n, pref)


def _trunk(x, mem, w):
    B, S, _ = x.shape
    T = B * S
    x2d = x.reshape(T, D_MODEL)
    proj = _inproj(x2d, w["g_mix"], w["w_in"], _tile(T, PROJ_TM))
    proj3 = proj.reshape(B, S, IN_COLS)
    o_f, o_b = _hgrn(proj3, w["lb"], _tile(S, HG_ROWS))
    o_da = _diff_attn(proj3, w["attn_scalars"], w["g_da"], _tile(S, ATT_TQ), _tile(S, ATT_TK), w["da_out_scale"])
    kv = _memkv(mem.reshape(B * N_MEM, D_MODEL), w["g_mem"], w["w_ckv"]).reshape(B, N_MEM, 2 * D_MODEL)
    tm = _tile(S, POST_TM)
    x2, hn, ri, rw, cnt = _post(x2d, o_f.reshape(T, HG_WIDTH), o_b.reshape(T, HG_WIDTH), proj,
                                o_da.reshape(T, DA_WIDTH), kv, S, w, tm)

    counts = cnt[0, N_GROUPS:N_GROUPS + N_EXPERTS].astype(jnp.int32)
    padded = (counts + MOE_BLOCK - 1) // MOE_BLOCK * MOE_BLOCK
    pad_end = jnp.cumsum(padded)
    pad_start = pad_end - padded
    dest = pad_start[ri[:, 0:2]] + ri[:, 2:4]
    n_slots = 2 * T + N_EXPERTS * MOE_BLOCK
    nb = n_slots // MOE_BLOCK
    block_expert = jnp.minimum(jnp.searchsorted(pad_end, jnp.arange(nb, dtype=jnp.int32) * MOE_BLOCK, side="right"),
                               N_EXPERTS - 1).astype(jnp.int32)
    n_used = (pad_end[-1:] // MOE_BLOCK).astype(jnp.int32)
    xs = _dispatch(hn, dest, n_slots, tm)
    ys = _experts(xs, block_expert, n_used, w["w_gate"], w["w_up"], w["w_down"])
    y = _combine(x2, rw, w["g_final"], ys, dest, tm)
    return y.reshape(B, S, D_MODEL)


def kernel(x_prompt, x_sample, mem_prompt, mem_sample, g_mix, w_in, hg_lb, g_hg, lam_q1, lam_k1, lam_q2, lam_k2,
           g_da, w_out, g_ca, g_mem, w_cq, w_ckv, w_co, g_ffn, w_rg, b_rg, w_re, b_re, w_gate, w_up, w_down, g_final):
    l = 0
    lam_init = 0.8 - 0.6 * math.exp(-0.3 * l)
    lam = (jnp.exp(jnp.sum(lam_q1[l] * lam_k1[l])) - jnp.exp(jnp.sum(lam_q2[l] * lam_k2[l])) + lam_init)
    slopes = jnp.power(2.0, -8.0 * jnp.arange(1, DA_HEADS + 1, dtype=F32) / DA_HEADS)
    pad = LANES - N_GROUPS - N_EXPERTS
    w = {
        "g_mix": g_mix[l][None], "w_in": w_in[l].astype(BF16),
        "lb": jnp.cumsum(jax.nn.softmax(hg_lb, axis=0), axis=0)[l],
        "g_hg": g_hg[l][None],
        "attn_scalars": jnp.concatenate([slopes, lam[None]]).astype(F32),
        "g_da": g_da[l][None], "da_out_scale": 1.0 - lam_init,
        "w_out": w_out[l].astype(BF16), "g_ca": g_ca[l][None], "g_mem": g_mem[l][None],
        "w_cq": w_cq[l].astype(BF16), "w_ckv": w_ckv[l].astype(BF16), "w_co": w_co[l].astype(BF16),
        "g_ffn": g_ffn[l][None],
        "w_r": jnp.pad(jnp.concatenate([w_rg[l], w_re[l]], axis=1), ((0, 0), (0, pad))),
        "b_r": jnp.pad(jnp.concatenate([b_rg[l], b_re[l]]), (0, pad))[None],
        "w_gate": w_gate[l].astype(BF16), "w_up": w_up[l].astype(BF16), "w_down": w_down[l].astype(BF16),
        "g_final": g_final[None],
    }
    return (_trunk(x_prompt, mem_prompt, w), _trunk(x_sample, mem_sample, w))
```

```python
import functools
import math

import jax
import jax.numpy as jnp
from jax import lax
from jax.experimental import pallas as pl
from jax.experimental.pallas import tpu as pltpu

F32 = jnp.float32
BF16 = jnp.bfloat16

D_MODEL = 1024
N_MEM = 256
EPS = 1e-6
HG_HEADS = 4
HG_D = 128
HG_WIDTH = HG_HEADS * HG_D
DA_HEADS = 4
DA_DK = 64
DA_DV = 128
DA_WIDTH = DA_HEADS * DA_DV
IN_COLS = 4096
CA_HEADS = 4
CA_DH = 256
N_GROUPS = 4
EXP_PER_GROUP = 8
N_EXPERTS = 32
D_EXPERT = 512
LANES = 128
VT_ROWS = DA_DV + 16

CW_HQ, CW_FF, CW_FB, CW_HI, CW_HGATE = 0, 1, 2, 3, 4
CB_DQ, CB_DK, CB_DV = 20, 24, 28

HG_CHUNK = 64
HG_SUB = 16
EXP_CLAMP = 80.0
MOE_BLOCK = 256

VMEM_LIMIT = 48 * 1024 * 1024

PROJ_TM = 512
HG_ROWS = 256
ATT_T = 512
POST_TM = 256


def _cparams(sem, **kw):
    return pltpu.CompilerParams(dimension_semantics=sem, vmem_limit_bytes=VMEM_LIMIT, **kw)


def _rms(x, g):
    return x * lax.rsqrt(jnp.mean(x * x, axis=-1, keepdims=True) + EPS) * g


def _silu(x):
    return x * (1.0 / (1.0 + jnp.exp(-x)))


def _inproj_kernel(x_ref, g_ref, w_ref, wvt_ref, o_ref, vt_ref, *, n_chunk):
    h = _rms(x_ref[...], g_ref[...]).astype(BF16)
    for j in range(IN_COLS // n_chunk):
        sl = slice(j * n_chunk, (j + 1) * n_chunk)
        o_ref[:, sl] = jnp.dot(h, w_ref[:, sl], preferred_element_type=F32).astype(BF16)
    vt = lax.dot_general(wvt_ref[...], h, (((1,), (1,)), ((), ())), preferred_element_type=F32).astype(BF16)
    for hd in range(DA_HEADS):
        vt_ref[hd * VT_ROWS:hd * VT_ROWS + DA_DV, :] = vt[hd * DA_DV:(hd + 1) * DA_DV]
        vt_ref[hd * VT_ROWS + DA_DV:(hd + 1) * VT_ROWS, :] = jnp.ones((VT_ROWS - DA_DV, vt.shape[1]), BF16)


def _inproj(x2d, g_mix, w_in_bf, w_vt_bf, tm):
    T = x2d.shape[0]
    return pl.pallas_call(
        functools.partial(_inproj_kernel, n_chunk=512),
        grid=(T // tm,),
        in_specs=[pl.BlockSpec((tm, D_MODEL), lambda i: (i, 0)),
                  pl.BlockSpec((1, D_MODEL), lambda i: (0, 0)),
                  pl.BlockSpec((D_MODEL, IN_COLS), lambda i: (0, 0)),
                  pl.BlockSpec((DA_WIDTH, D_MODEL), lambda i: (0, 0))],
        out_specs=[pl.BlockSpec((tm, IN_COLS), lambda i: (i, 0)),
                   pl.BlockSpec((DA_HEADS * VT_ROWS, tm), lambda i: (0, i))],
        out_shape=[jax.ShapeDtypeStruct((T, IN_COLS), BF16), jax.ShapeDtypeStruct((DA_HEADS * VT_ROWS, T), BF16)],
        compiler_params=_cparams(("parallel",)),
        name="inproj",
    )(x2d, g_mix, w_in_bf, w_vt_bf)


def _sigmoid(x):
    return 0.5 + 0.5 * jnp.tanh(0.5 * x)


def _hgrn_step(dirs, lb, st_refs, tri2, keep):
    C, S = HG_CHUNK, HG_SUB
    n_sub = C // S
    nt = (((1,), (1,)), ((), ()))
    stage1 = []
    for d, (q_raw, f_raw, v) in enumerate(dirs):
        xq = q_raw.astype(F32)
        q = xq * _sigmoid(xq)
        f = lb + (1.0 - lb) * _sigmoid(f_raw.astype(F32))
        lf = jnp.log(f)
        hi = lf.astype(BF16)
        mid = (lf - hi.astype(F32)).astype(BF16)
        cc = jnp.dot(tri2[d], hi, preferred_element_type=F32) + jnp.dot(tri2[d], mid, preferred_element_type=F32)
        stage1.append((q, 1.0 - f, v, cc[:C], cc[C:]))
    stage2 = []
    for d, (q, k, v, cum, cum_loc) in enumerate(stage1):
        ref_pt = cum - cum_loc
        edge = cum[0:1, :] if d == 1 else cum[C - 1:C, :]
        q_inter = (q * jnp.exp(cum)).astype(BF16)
        q_loc = (q * jnp.exp(cum_loc)).astype(BF16)
        k_st = (k * jnp.exp(edge - cum)).astype(BF16)
        k_sub = [(k * jnp.exp(jnp.minimum(ref_pt[i * S:i * S + 1, :] - cum, EXP_CLAMP))).astype(BF16)
                 for i in range(n_sub)]
        stage2.append((q_inter, q_loc, k_st, k_sub, jnp.exp(edge), v))
    chains = [(d, h) for d in range(2) for h in range(HG_HEADS)]
    hs = lambda h: slice(h * HG_D, (h + 1) * HG_D)
    st_old, o_inter, scores = {}, {}, {}
    for d, h in chains:
        q_inter, q_loc, _, k_sub, _, _ = stage2[d]
        st_old[d, h] = st_refs[d][h]
        o_inter[d, h] = lax.dot_general(q_inter[:, hs(h)], st_old[d, h].astype(BF16), nt, preferred_element_type=F32)
        sc = jnp.concatenate([lax.dot_general(q_loc[i * S:(i + 1) * S, hs(h)], k_sub[i][:, hs(h)], nt,
                                              preferred_element_type=F32) for i in range(n_sub)], axis=0)
        scores[d, h] = jnp.where(keep[d], sc, 0.0).astype(BF16)
    outs = {}
    for d, h in chains:
        outs[d, h] = o_inter[d, h] + jnp.dot(scores[d, h], stage2[d][5][:, hs(h)], preferred_element_type=F32)
    for d, h in chains:
        _, _, k_st, _, dec, v = stage2[d]
        st_refs[d][h] = st_old[d, h] * dec[:, hs(h)] + lax.dot_general(
            v[:, hs(h)], k_st[:, hs(h)], (((0,), (0,)), ((), ())), preferred_element_type=F32)
    return [jnp.concatenate([outs[d, h] for h in range(HG_HEADS)], axis=-1) for d in range(2)]


def _hgrn_kernel(lb_ref, qf_ref, ff_ref, vf_ref, qb_ref, fb_ref, vb_ref, of_ref, ob_ref, stf_ref, stb_ref, *, rows):
    @pl.when(pl.program_id(1) == 0)
    def _():
        stf_ref[...] = jnp.zeros_like(stf_ref)
        stb_ref[...] = jnp.zeros_like(stb_ref)

    C, S = HG_CHUNK, HG_SUB
    r = lax.broadcasted_iota(jnp.int32, (2 * C, C), 0)
    c = lax.broadcasted_iota(jnp.int32, (2 * C, C), 1)
    rr = jnp.where(r >= C, r - C, r)
    local = (r < C) | ((rr // S) == (c // S))
    tri2 = [jnp.where((c <= rr) & local, 1.0, 0.0).astype(BF16), jnp.where((c >= rr) & local, 1.0, 0.0).astype(BF16)]
    r1 = lax.broadcasted_iota(jnp.int32, (C, C), 0)
    c1 = lax.broadcasted_iota(jnp.int32, (C, C), 1)
    keep = [c1 <= r1, c1 >= r1]
    lb = lb_ref[...]
    n = rows // C

    def body(j, carry):
        a = pl.multiple_of(j * C, C)
        b = pl.multiple_of((n - 1 - j) * C, C)
        dirs = [(qf_ref[pl.ds(a, C), :], ff_ref[pl.ds(a, C), :], vf_ref[pl.ds(a, C), :]),
                (qb_ref[pl.ds(b, C), :], fb_ref[pl.ds(b, C), :], vb_ref[pl.ds(b, C), :])]
        o_f, o_b = _hgrn_step(dirs, lb, (stf_ref, stb_ref), tri2, keep)
        of_ref[pl.ds(a, C), :] = o_f.astype(of_ref.dtype)
        ob_ref[pl.ds(b, C), :] = o_b.astype(ob_ref.dtype)
        return carry

    lax.fori_loop(0, n, body, 0)


def _hgrn(proj3, lb, rows):
    B, S, _ = proj3.shape
    n = S // rows
    W = HG_WIDTH

    def fwd(cb):
        return pl.BlockSpec((None, rows, W), lambda b, c: (b, c, cb))

    def bwd(cb):
        return pl.BlockSpec((None, rows, W), lambda b, c: (b, n - 1 - c, cb))

    out = jax.ShapeDtypeStruct((B, S, W), BF16)
    return pl.pallas_call(
        functools.partial(_hgrn_kernel, rows=rows),
        grid=(B, n),
        in_specs=[pl.BlockSpec((1, W), lambda b, c: (0, 0)),
                  fwd(CW_HQ), fwd(CW_FF), fwd(CW_HI), bwd(CW_HQ), bwd(CW_FB), bwd(CW_HI)],
        out_specs=[pl.BlockSpec((None, rows, W), lambda b, c: (b, c, 0)),
                   pl.BlockSpec((None, rows, W), lambda b, c: (b, n - 1 - c, 0))],
        out_shape=[out, out],
        scratch_shapes=[pltpu.VMEM((HG_HEADS, HG_D, HG_D), F32), pltpu.VMEM((HG_HEADS, HG_D, HG_D), F32)],
        compiler_params=_cparams(("parallel", "arbitrary")),
        name="hgrn2",
    )(lb.reshape(1, W), proj3, proj3, proj3, proj3, proj3, proj3)


LOG2E = 1.4426950408889634
N_FEAT = 6


def _attn_chain(s, colterm, vt, m_ref, acc_ref, c, cols):
    m_old = m_ref[c, :, cols]
    m_new = jnp.maximum(m_old, jnp.max(s, axis=0, keepdims=True) + colterm)
    alpha = jnp.exp2(m_old - m_new)
    p = jnp.exp2(s - (m_new - colterm)).astype(BF16)
    acc_ref[c, :, cols] = alpha * acc_ref[c, :, cols] + jnp.dot(vt, p, preferred_element_type=F32)
    m_ref[c, :, cols] = m_new


_ATT_CHAINS = [(c, hf) for c in range(2) for hf in range(2)]
_NT = (((1,), (1,)), ((), ()))


def _attn_kernel(sc_ref, q_ref, k_ref, vt_ref, g_ref, o_ref, qa_ref, ka_ref, e_ref, s0_ref, s1_ref,
                 m_ref, acc_ref, *, t, n, out_scale):
    h, qi = pl.program_id(1), pl.program_id(2)
    slope2 = sc_ref[h] * LOG2E
    half = t // 2
    lane = lax.broadcasted_iota(jnp.int32, (t, LANES), 1)
    row = lax.broadcasted_iota(jnp.int32, (t, LANES), 0)
    feat_lane = [lane - (DA_DK if c == 0 else 0) for c in range(2)]
    is_feat = [(fl >= 0) & (fl < N_FEAT) for fl in feat_lane]

    @pl.when(qi == 0)
    def _():
        r_hi = ((row >> 4) << 4).astype(F32)
        r_lo = (row & 15).astype(F32)
        for c in range(2):
            fk = jnp.where((feat_lane[c] & 1) == 0, r_hi, r_lo).astype(BF16)
            for j in range(n):
                ka_ref[c, j * t:(j + 1) * t, :] = jnp.where(is_feat[c], fk, k_ref[j * t:(j + 1) * t, :])

    m_ref[...] = jnp.full_like(m_ref, -jnp.inf)
    acc_ref[...] = jnp.zeros_like(acc_ref)
    q = q_ref[...].astype(F32) * (DA_DK ** -0.5 * LOG2E)
    sv = jnp.full((t, LANES), slope2, F32)
    s_hi = sv.astype(BF16).astype(F32)
    s_mid = (sv - s_hi).astype(BF16).astype(F32)
    s_lo = (sv - s_hi - s_mid).astype(BF16).astype(F32)
    for c in range(2):
        fl = feat_lane[c]
        own = (lane < DA_DK) == (c == 0)
        feat_q = jnp.where(is_feat[c], jnp.where(fl < 2, s_hi, jnp.where(fl < 4, s_mid, s_lo)), 0.0)
        qa_ref[0 + c] = jnp.where(own, q, feat_q).astype(BF16)
        qa_ref[2 + c] = jnp.where(own, q, -feat_q).astype(BF16)
        qa_ref[4 + c] = jnp.where(own, q, 0.0).astype(BF16)
    rel = lax.broadcasted_iota(jnp.int32, (t, t), 1) - lax.broadcasted_iota(jnp.int32, (t, t), 0)
    e_ref[...] = jnp.abs(rel).astype(F32) * (-slope2)

    def tile_of(u):
        return jnp.where(u == 0, qi, u - 1 + (u - 1 >= qi).astype(jnp.int32))

    def scores_to(slot, u):
        j = tile_of(u)
        qsel = jnp.where(j == qi, 4, jnp.where(j < qi, 0, 2))
        r0 = pl.multiple_of(j * t, t)
        for c, hf in _ATT_CHAINS:
            qa = qa_ref[qsel + c, hf * half:(hf + 1) * half, :]
            slot[c, :, hf * half:(hf + 1) * half] = lax.dot_general(ka_ref[c, pl.ds(r0, t), :], qa, _NT,
                                                                      preferred_element_type=F32)

    def consume(slot, u, diagonal=False):
        j = tile_of(u)
        vt = vt_ref[:, pl.ds(pl.multiple_of(j * t, t), t)]
        if diagonal:
            colterm = jnp.zeros((1, t), F32)
        else:
            sign = jnp.where(j < qi, -slope2, slope2)
            col = lax.broadcasted_iota(jnp.int32, (1, t), 1) + (qi - j) * t
            colterm = col.astype(F32) * sign
        for c, hf in _ATT_CHAINS:
            cols = slice(hf * half, (hf + 1) * half)
            s = slot[c, :, cols]
            if diagonal:
                s = s + e_ref[:, cols]
            _attn_chain(s, colterm[:, cols], vt, m_ref, acc_ref, c, cols)

    i32 = jnp.int32
    scores_to(s0_ref, i32(0))
    if n > 1:
        scores_to(s1_ref, i32(1))
    consume(s0_ref, i32(0), diagonal=True)
    if n > 1:
        trips = (n - 2) // 2

        def pair(p, carry):
            u = 1 + 2 * p
            scores_to(s0_ref, u + 1)
            consume(s1_ref, u)
            scores_to(s1_ref, u + 2)
            consume(s0_ref, u + 1)
            return carry

        lax.fori_loop(0, trips, pair, 0)
        u_last = 1 + 2 * trips
        if u_last == n - 1:
            consume(s1_ref, i32(u_last))
        else:
            scores_to(s0_ref, i32(n - 1))
            consume(s1_ref, i32(u_last))
            consume(s0_ref, i32(n - 1))

    lam = sc_ref[DA_HEADS]
    a0, a1 = acc_ref[0], acc_ref[1]
    o = a0[:DA_DV] / a0[DA_DV:DA_DV + 1] - lam * (a1[:DA_DV] / a1[DA_DV:DA_DV + 1])
    o = o * lax.rsqrt(jnp.mean(o * o, axis=0, keepdims=True) + EPS) * (g_ref[...] * out_scale)
    o_ref[...] = o.T.astype(o_ref.dtype)


def _diff_attn(proj3, vt, scalars, g_da_col, t, out_scale):
    B, S, _ = proj3.shape
    n = S // t
    return pl.pallas_call(
        functools.partial(_attn_kernel, t=t, n=n, out_scale=out_scale),
        grid_spec=pltpu.PrefetchScalarGridSpec(
            num_scalar_prefetch=1,
            grid=(B, DA_HEADS, n),
            in_specs=[pl.BlockSpec((None, t, LANES), lambda b, h, i, sc: (b, i, CB_DQ + h)),
                      pl.BlockSpec((None, S, LANES), lambda b, h, i, sc: (b, 0, CB_DK + h)),
                      pl.BlockSpec((VT_ROWS, S), lambda b, h, i, sc: (h, b)),
                      pl.BlockSpec((DA_DV, 1), lambda b, h, i, sc: (0, 0))],
            out_specs=pl.BlockSpec((None, t, LANES), lambda b, h, i, sc: (b, i, h)),
            scratch_shapes=[pltpu.VMEM((6, t, LANES), BF16), pltpu.VMEM((2, S, LANES), BF16),
                            pltpu.VMEM((t, t), F32), pltpu.VMEM((2, t, t), F32), pltpu.VMEM((2, t, t), F32),
                            pltpu.VMEM((2, 1, t), F32), pltpu.VMEM((2, VT_ROWS, t), F32)]),
        out_shape=jax.ShapeDtypeStruct((B, S, DA_WIDTH), BF16),
        compiler_params=_cparams(("parallel", "parallel", "arbitrary")),
        name="diff_attn",
    )(scalars, proj3, proj3, vt, g_da_col)


def _memkv_kernel(m_ref, g_ref, w_ref, o_ref):
    h = _rms(m_ref[...], g_ref[...]).astype(BF16)
    o_ref[...] = jnp.dot(h, w_ref[...], preferred_element_type=F32).astype(BF16)


def _memkv(mem2d, g_mem, w_ckv_bf):
    T = mem2d.shape[0]
    return pl.pallas_call(
        _memkv_kernel,
        grid=(T // N_MEM,),
        in_specs=[pl.BlockSpec((N_MEM, D_MODEL), lambda i: (i, 0)),
                  pl.BlockSpec((1, D_MODEL), lambda i: (0, 0)),
                  pl.BlockSpec((D_MODEL, 2 * D_MODEL), lambda i: (0, 0))],
        out_specs=pl.BlockSpec((N_MEM, 2 * D_MODEL), lambda i: (i, 0)),
        out_shape=jax.ShapeDtypeStruct((T, 2 * D_MODEL), BF16),
        compiler_params=_cparams(("parallel",)),
        name="mem_kv",
    )(mem2d, g_mem, w_ckv_bf)


def _post_kernel(x_ref, of_ref, ob_ref, hgate_ref, oda_ref, kv_ref, ghg_ref, wout_ref, gca_ref, wcq_ref, wco_ref,
                 gffn_ref, wr_ref, br_ref,
                 x2_ref, hn_ref, ri_ref, rw_ref, cnt_ref, carry_ref, *, tm):
    @pl.when(pl.program_id(0) == 0)
    def _():
        carry_ref[...] = jnp.zeros_like(carry_ref)

    o_hg = of_ref[...].astype(F32) + ob_ref[...].astype(F32)
    ghg = ghg_ref[...]
    parts = [_rms(o_hg[:, i * HG_D:(i + 1) * HG_D], ghg[:, i * HG_D:(i + 1) * HG_D]) for i in range(HG_HEADS)]
    mix_hg = (jnp.concatenate(parts, axis=-1) * _silu(hgate_ref[...].astype(F32))).astype(BF16)
    x1 = (x_ref[...] + jnp.dot(mix_hg, wout_ref[:HG_WIDTH, :], preferred_element_type=F32)
          + jnp.dot(oda_ref[...], wout_ref[HG_WIDTH:, :], preferred_element_type=F32))

    hq = jnp.dot(_rms(x1, gca_ref[...]).astype(BF16), wcq_ref[...], preferred_element_type=F32).astype(BF16)
    heads = []
    for i in range(CA_HEADS):
        kh = kv_ref[:, i * CA_DH:(i + 1) * CA_DH]
        vh = kv_ref[:, D_MODEL + i * CA_DH:D_MODEL + (i + 1) * CA_DH]
        s = lax.dot_general(hq[:, i * CA_DH:(i + 1) * CA_DH], kh, (((1,), (1,)), ((), ())),
                            preferred_element_type=F32) * (CA_DH ** -0.5)
        p = jnp.exp(s - jnp.max(s, axis=-1, keepdims=True))
        p = (p / jnp.sum(p, axis=-1, keepdims=True)).astype(BF16)
        heads.append(jnp.dot(p, vh, preferred_element_type=F32).astype(BF16))
    x2 = x1 + jnp.dot(jnp.concatenate(heads, axis=-1), wco_ref[...], preferred_element_type=F32)
    x2_ref[...] = x2

    hn = _rms(x2, gffn_ref[...])
    hn_ref[...] = hn
    hn_hi = hn.astype(BF16)
    hn_lo = (hn - hn_hi.astype(F32)).astype(BF16)
    logits = (jnp.dot(hn_hi, wr_ref[0], preferred_element_type=F32) + jnp.dot(hn_lo, wr_ref[0], preferred_element_type=F32)
              + jnp.dot(hn_hi, wr_ref[1], preferred_element_type=F32) + br_ref[...])
    lane = lax.broadcasted_iota(jnp.int32, (tm, LANES), 1)
    big = jnp.int32(1 << 20)
    neg = jnp.float32(-jnp.inf)

    def first_max(vals):
        m = jnp.max(vals, axis=-1, keepdims=True)
        return m, jnp.min(jnp.where(vals == m, lane, big), axis=-1, keepdims=True)

    gl = jnp.where(lane < N_GROUPS, logits, neg)
    gmax, gidx = first_max(gl)
    g_val = 1.0 / jnp.sum(jnp.exp(gl - gmax), axis=-1, keepdims=True)
    e_lane = lane - N_GROUPS
    in_grp = (e_lane >= 0) & (e_lane < N_EXPERTS) & ((e_lane // EXP_PER_GROUP) == gidx)
    el = jnp.where(in_grp, logits, neg)
    m1, i1 = first_max(el)
    m2, i2 = first_max(jnp.where(lane == i1, neg, el))
    t = jnp.exp(m2 - m1)
    w1 = g_val / (1.0 + t)
    w2 = g_val * t / (1.0 + t)

    hit1, hit2 = lane == i1, lane == i2
    onehot = (hit1 | hit2).astype(BF16)
    r_i = lax.broadcasted_iota(jnp.int32, (tm, tm), 0)
    c_i = lax.broadcasted_iota(jnp.int32, (tm, tm), 1)
    before = jnp.dot((c_i < r_i).astype(BF16), onehot, preferred_element_type=F32) + carry_ref[...]
    r1 = jnp.sum(jnp.where(hit1, before, 0.0), axis=-1, keepdims=True).astype(jnp.int32)
    r2 = jnp.sum(jnp.where(hit2, before, 0.0), axis=-1, keepdims=True).astype(jnp.int32)
    carry_ref[...] += jnp.sum(onehot.astype(F32), axis=0, keepdims=True)
    cnt_ref[...] = carry_ref[...]
    ri_ref[...] = jnp.where(lane == 0, i1 - N_GROUPS, jnp.where(lane == 1, i2 - N_GROUPS,
                            jnp.where(lane == 2, r1, jnp.where(lane == 3, r2, 0))))
    rw_ref[...] = jnp.where(lane == 0, w1, jnp.where(lane == 1, w2, 0.0))


def _post(x2d, o_f, o_b, proj, o_da, kv3, S, w, tm):
    T = x2d.shape[0]
    per_b = S // tm
    row = lambda i: (i, 0)
    const = lambda i: (0, 0)
    return pl.pallas_call(
        functools.partial(_post_kernel, tm=tm),
        grid=(T // tm,),
        in_specs=[pl.BlockSpec((tm, D_MODEL), row),
                  pl.BlockSpec((tm, HG_WIDTH), row),
                  pl.BlockSpec((tm, HG_WIDTH), row),
                  pl.BlockSpec((tm, HG_WIDTH), lambda i: (i, CW_HGATE)),
                  pl.BlockSpec((tm, DA_WIDTH), row),
                  pl.BlockSpec((None, N_MEM, 2 * D_MODEL), lambda i: (i // per_b, 0, 0)),
                  pl.BlockSpec((1, HG_WIDTH), const),
                  pl.BlockSpec((D_MODEL, D_MODEL), const),
                  pl.BlockSpec((1, D_MODEL), const),
                  pl.BlockSpec((D_MODEL, D_MODEL), const),
                  pl.BlockSpec((D_MODEL, D_MODEL), const),
                  pl.BlockSpec((1, D_MODEL), const),
                  pl.BlockSpec((2, D_MODEL, LANES), lambda i: (0, 0, 0)),
                  pl.BlockSpec((1, LANES), const)],
        out_specs=[pl.BlockSpec((tm, D_MODEL), row),
                   pl.BlockSpec((tm, D_MODEL), row),
                   pl.BlockSpec((tm, LANES), row),
                   pl.BlockSpec((tm, LANES), row),
                   pl.BlockSpec((1, LANES), const)],
        out_shape=[jax.ShapeDtypeStruct((T, D_MODEL), F32),
                   jax.ShapeDtypeStruct((T, D_MODEL), F32),
                   jax.ShapeDtypeStruct((T, LANES), jnp.int32),
                   jax.ShapeDtypeStruct((T, LANES), F32),
                   jax.ShapeDtypeStruct((1, LANES), F32)],
        scratch_shapes=[pltpu.VMEM((1, LANES), F32)],
        compiler_params=_cparams(("arbitrary",)),
        name="post_mixer",
    )(x2d, o_f, o_b, proj, o_da, kv3, w["g_hg"], w["w_out"], w["g_ca"], w["w_cq"], w["w_co"], w["g_ffn"],
      w["w_r"], w["b_r"])


def _dispatch_kernel(dest_ref, hn_ref, xs_in_ref, xs_ref, sem, *, tm):
    del xs_in_ref

    def copy(t, kk):
        return pltpu.make_async_copy(hn_ref.at[pl.ds(t, 1), :],
                                     xs_ref.at[pl.ds(dest_ref[0, 2 * t + kk], 1), :], sem)

    def start(t, c):
        copy(t, 0).start()
        copy(t, 1).start()
        return c

    def wait(t, c):
        copy(t, 0).wait()
        copy(t, 1).wait()
        return c

    lax.fori_loop(0, tm, start, 0)
    lax.fori_loop(0, tm, wait, 0)


def _dispatch(hn, dest, n_slots, tm):
    T = hn.shape[0]
    xs0 = jnp.zeros((n_slots, D_MODEL), F32)
    return pl.pallas_call(
        functools.partial(_dispatch_kernel, tm=tm),
        grid=(T // tm,),
        in_specs=[pl.BlockSpec((None, 1, 2 * tm), lambda i: (i, 0, 0), memory_space=pltpu.SMEM),
                  pl.BlockSpec((tm, D_MODEL), lambda i: (i, 0)),
                  pl.BlockSpec(memory_space=pl.ANY)],
        out_specs=pl.BlockSpec(memory_space=pl.ANY),
        out_shape=jax.ShapeDtypeStruct((n_slots, D_MODEL), F32),
        scratch_shapes=[pltpu.SemaphoreType.DMA(())],
        input_output_aliases={2: 0},
        compiler_params=_cparams(("arbitrary",), has_side_effects=True),
        name="moe_dispatch",
    )(dest.reshape(T // tm, 1, 2 * tm), hn, xs0)


def _expert_kernel(be_ref, nu_ref, xs_ref, wg_ref, wu_ref, wd_ref, ys_ref):
    del be_ref

    @pl.when(pl.program_id(0) < nu_ref[0])
    def _():
        xb = xs_ref[...].astype(BF16)
        a = jnp.dot(xb, wg_ref[...], preferred_element_type=F32)
        u = jnp.dot(xb, wu_ref[...], preferred_element_type=F32)
        hmid = (_silu(a) * u).astype(BF16)
        ys_ref[...] = jnp.dot(hmid, wd_ref[...], preferred_element_type=F32)

    @pl.when(pl.program_id(0) >= nu_ref[0])
    def _():
        ys_ref[...] = jnp.zeros_like(ys_ref)


def _experts(xs, block_expert, n_used, w_gate, w_up, w_down):
    n_slots = xs.shape[0]
    nb = n_slots // MOE_BLOCK
    return pl.pallas_call(
        _expert_kernel,
        grid_spec=pltpu.PrefetchScalarGridSpec(
            num_scalar_prefetch=2,
            grid=(nb,),
            in_specs=[pl.BlockSpec((MOE_BLOCK, D_MODEL), lambda i, be, nu: (i, 0)),
                      pl.BlockSpec((None, D_MODEL, D_EXPERT), lambda i, be, nu: (be[i], 0, 0)),
                      pl.BlockSpec((None, D_MODEL, D_EXPERT), lambda i, be, nu: (be[i], 0, 0)),
                      pl.BlockSpec((None, D_EXPERT, D_MODEL), lambda i, be, nu: (be[i], 0, 0))],
            out_specs=pl.BlockSpec((MOE_BLOCK, D_MODEL), lambda i, be, nu: (i, 0))),
        out_shape=jax.ShapeDtypeStruct((n_slots, D_MODEL), F32),
        compiler_params=_cparams(("arbitrary",)),
        name="moe_experts",
    )(block_expert, n_used, xs, w_gate, w_up, w_down)


def _combine_kernel(dest_ref, x2_ref, rw_ref, g_ref, ys_ref, y_ref, buf_ref, sem, *, tm):
    def copy(t, kk):
        return pltpu.make_async_copy(ys_ref.at[pl.ds(dest_ref[0, 2 * t + kk], 1), :],
                                     buf_ref.at[kk, pl.ds(t, 1), :], sem)

    def start(t, c):
        copy(t, 0).start()
        copy(t, 1).start()
        return c

    def wait(t, c):
        copy(t, 0).wait()
        copy(t, 1).wait()
        return c

    lax.fori_loop(0, tm, start, 0)
    lax.fori_loop(0, tm, wait, 0)
    rw = rw_ref[...]
    lane = lax.broadcasted_iota(jnp.int32, rw.shape, 1)
    w1 = jnp.sum(jnp.where(lane == 0, rw, 0.0), axis=-1, keepdims=True)
    w2 = jnp.sum(jnp.where(lane == 1, rw, 0.0), axis=-1, keepdims=True)
    x3 = x2_ref[...] + (buf_ref[0] * w1 + buf_ref[1] * w2)
    y_ref[...] = _rms(x3, g_ref[...])


def _combine(x2, rw, g_final, ys, dest, tm):
    T = x2.shape[0]
    return pl.pallas_call(
        functools.partial(_combine_kernel, tm=tm),
        grid=(T // tm,),
        in_specs=[pl.BlockSpec((None, 1, 2 * tm), lambda i: (i, 0, 0), memory_space=pltpu.SMEM),
                  pl.BlockSpec((tm, D_MODEL), lambda i: (i, 0)),
                  pl.BlockSpec((tm, LANES), lambda i: (i, 0)),
                  pl.BlockSpec((1, D_MODEL), lambda i: (0, 0)),
                  pl.BlockSpec(memory_space=pl.ANY)],
        out_specs=pl.BlockSpec((tm, D_MODEL), lambda i: (i, 0)),
        out_shape=jax.ShapeDtypeStruct((T, D_MODEL), F32),
        scratch_shapes=[pltpu.VMEM((2, tm, D_MODEL), F32), pltpu.SemaphoreType.DMA(())],
        compiler_params=_cparams(("arbitrary",)),
        name="moe_combine",
    )(dest.reshape(T // tm, 1, 2 * tm), x2, rw, g_final, ys)


def _tile(n, pref):
    return min(n, pref)


def _trunk(x, mem, w):
    B, S, _ = x.shape
    T = B * S
    x2d = x.reshape(T, D_MODEL)
    proj, vt = _inproj(x2d, w["g_mix"], w["w_in"], w["w_vt"], _tile(T, PROJ_TM))
    proj3 = proj.reshape(B, S, IN_COLS)
    o_f, o_b = _hgrn(proj3, w["lb"], _tile(S, HG_ROWS))
    o_da = _diff_attn(proj3, vt, w["attn_scalars"], w["g_da"], _tile(S, ATT_T), w["da_out_scale"])
    kv = _memkv(mem.reshape(B * N_MEM, D_MODEL), w["g_mem"], w["w_ckv"]).reshape(B, N_MEM, 2 * D_MODEL)
    tm = _tile(S, POST_TM)
    x2, hn, ri, rw, cnt = _post(x2d, o_f.reshape(T, HG_WIDTH), o_b.reshape(T, HG_WIDTH), proj,
                                o_da.reshape(T, DA_WIDTH), kv, S, w, tm)

    counts = cnt[0, N_GROUPS:N_GROUPS + N_EXPERTS].astype(jnp.int32)
    padded = (counts + MOE_BLOCK - 1) // MOE_BLOCK * MOE_BLOCK
    pad_end = jnp.cumsum(padded)
    pad_start = pad_end - padded
    dest = pad_start[ri[:, 0:2]] + ri[:, 2:4]
    n_slots = 2 * T + N_EXPERTS * MOE_BLOCK
    nb = n_slots // MOE_BLOCK
    block_start = jnp.arange(nb, dtype=jnp.int32) * MOE_BLOCK
    block_expert = jnp.minimum(jnp.sum((pad_end[None, :] <= block_start[:, None]).astype(jnp.int32), axis=1),
                               N_EXPERTS - 1)
    n_used = (pad_end[-1:] // MOE_BLOCK).astype(jnp.int32)
    xs = _dispatch(hn, dest, n_slots, tm)
    ys = _experts(xs, block_expert, n_used, w["w_gate"], w["w_up"], w["w_down"])
    y = _combine(x2, rw, w["g_final"], ys, dest, tm)
    return y.reshape(B, S, D_MODEL)


def _split_bf16(a):
    hi = a.astype(BF16)
    return jnp.stack([hi, (a - hi.astype(F32)).astype(BF16)])


def kernel(x_prompt, x_sample, mem_prompt, mem_sample, g_mix, w_in, hg_lb, g_hg, lam_q1, lam_k1, lam_q2, lam_k2,
           g_da, w_out, g_ca, g_mem, w_cq, w_ckv, w_co, g_ffn, w_rg, b_rg, w_re, b_re, w_gate, w_up, w_down, g_final):
    l = 0
    lam_init = 0.8 - 0.6 * math.exp(-0.3 * l)
    lam = (jnp.exp(jnp.sum(lam_q1[l] * lam_k1[l])) - jnp.exp(jnp.sum(lam_q2[l] * lam_k2[l])) + lam_init)
    slopes = jnp.power(2.0, -8.0 * jnp.arange(1, DA_HEADS + 1, dtype=F32) / DA_HEADS)
    pad = LANES - N_GROUPS - N_EXPERTS
    w = {
        "g_mix": g_mix[l][None], "w_in": w_in[l].astype(BF16),
        "w_vt": w_in[l][:, CB_DV * LANES:].T.astype(BF16),
        "lb": jnp.cumsum(jax.nn.softmax(hg_lb, axis=0), axis=0)[l],
        "g_hg": g_hg[l][None],
        "attn_scalars": jnp.concatenate([slopes, lam[None]]).astype(F32),
        "g_da": g_da[l][:, None], "da_out_scale": 1.0 - lam_init,
        "w_out": w_out[l].astype(BF16), "g_ca": g_ca[l][None], "g_mem": g_mem[l][None],
        "w_cq": w_cq[l].astype(BF16), "w_ckv": w_ckv[l].astype(BF16), "w_co": w_co[l].astype(BF16),
        "g_ffn": g_ffn[l][None],
        "w_r": _split_bf16(jnp.pad(jnp.concatenate([w_rg[l], w_re[l]], axis=1), ((0, 0), (0, pad)))),
        "b_r": jnp.pad(jnp.concatenate([b_rg[l], b_re[l]]), (0, pad))[None],
        "w_gate": w_gate[l].astype(BF16), "w_up": w_up[l].astype(BF16), "w_down": w_down[l].astype(BF16),
        "g_final": g_final[None],
    }
    return (_trunk(x_prompt, mem_prompt, w), _trunk(x_sample, mem_sample, w))
```

```python
import functools
import math

import jax
import jax.numpy as jnp
from jax import lax
from jax.experimental import pallas as pl
from jax.experimental.pallas import tpu as pltpu

F32 = jnp.float32
BF16 = jnp.bfloat16

D_MODEL = 1024
N_MEM = 256
EPS = 1e-6
HG_HEADS = 4
HG_D = 128
HG_WIDTH = HG_HEADS * HG_D
DA_HEADS = 4
DA_DK = 64
DA_DV = 128
DA_WIDTH = DA_HEADS * DA_DV
IN_COLS = 4096
CA_HEADS = 4
CA_DH = 256
N_GROUPS = 4
EXP_PER_GROUP = 8
N_EXPERTS = 32
D_EXPERT = 512
LANES = 128
D_PACK = D_MODEL // 2
VT_ROWS = DA_DV + 16

CW_HQ, CW_FF, CW_FB, CW_HI, CW_HGATE = 0, 1, 2, 3, 4
CB_DQ, CB_DK, CB_DV = 20, 24, 28

HG_CHUNK = 64
HG_SUB = 16
EXP_CLAMP = 80.0
MOE_BLOCK = 256

VMEM_LIMIT = 48 * 1024 * 1024

PROJ_TM = 512
HG_ROWS = 256
ATT_T = 512
POST_TM = 512


def _cparams(sem, **kw):
    return pltpu.CompilerParams(dimension_semantics=sem, vmem_limit_bytes=VMEM_LIMIT, **kw)


def _rms(x, g):
    return x * lax.rsqrt(jnp.mean(x * x, axis=-1, keepdims=True) + EPS) * g


def _silu(x):
    return x * (1.0 / (1.0 + jnp.exp(-x)))


def _pack_bf16_pairs(x):
    n = x.shape[1] // 2
    lo = lax.bitcast_convert_type(x[:, :n].astype(BF16).astype(F32), jnp.uint32)
    hi = lax.bitcast_convert_type(x[:, n:].astype(BF16).astype(F32), jnp.uint32)
    return (lo >> 16) | (hi & jnp.uint32(0xFFFF0000))


def _unpack_bf16_pairs(w):
    lo = lax.bitcast_convert_type(w << 16, F32)
    hi = lax.bitcast_convert_type(w & jnp.uint32(0xFFFF0000), F32)
    return jnp.concatenate([lo, hi], axis=1)


def _inproj_kernel(x_ref, g_ref, w_ref, wvt_ref, o_ref, vt_ref, *, n_chunk):
    h = _rms(x_ref[...], g_ref[...]).astype(BF16)
    for j in range(IN_COLS // n_chunk):
        sl = slice(j * n_chunk, (j + 1) * n_chunk)
        o_ref[:, sl] = jnp.dot(h, w_ref[:, sl], preferred_element_type=F32).astype(BF16)
    vt = lax.dot_general(wvt_ref[...], h, (((1,), (1,)), ((), ())), preferred_element_type=F32).astype(BF16)
    for hd in range(DA_HEADS):
        vt_ref[hd * VT_ROWS:hd * VT_ROWS + DA_DV, :] = vt[hd * DA_DV:(hd + 1) * DA_DV]
        vt_ref[hd * VT_ROWS + DA_DV:(hd + 1) * VT_ROWS, :] = jnp.ones((VT_ROWS - DA_DV, vt.shape[1]), BF16)


def _inproj(x2d, g_mix, w_in_bf, w_vt_bf, tm):
    T = x2d.shape[0]
    return pl.pallas_call(
        functools.partial(_inproj_kernel, n_chunk=512),
        grid=(T // tm,),
        in_specs=[pl.BlockSpec((tm, D_MODEL), lambda i: (i, 0)),
                  pl.BlockSpec((1, D_MODEL), lambda i: (0, 0)),
                  pl.BlockSpec((D_MODEL, IN_COLS), lambda i: (0, 0)),
                  pl.BlockSpec((DA_WIDTH, D_MODEL), lambda i: (0, 0))],
        out_specs=[pl.BlockSpec((tm, IN_COLS), lambda i: (i, 0)),
                   pl.BlockSpec((DA_HEADS * VT_ROWS, tm), lambda i: (0, i))],
        out_shape=[jax.ShapeDtypeStruct((T, IN_COLS), BF16), jax.ShapeDtypeStruct((DA_HEADS * VT_ROWS, T), BF16)],
        compiler_params=_cparams(("parallel",)),
        name="inproj",
    )(x2d, g_mix, w_in_bf, w_vt_bf)


def _sigmoid(x):
    return 0.5 + 0.5 * jnp.tanh(0.5 * x)


def _hgrn_step(dirs, lb, st_refs, tri2, keep):
    C, S = HG_CHUNK, HG_SUB
    n_sub = C // S
    nt = (((1,), (1,)), ((), ()))
    stage1 = []
    for d, (q_raw, f_raw, v) in enumerate(dirs):
        xq = q_raw.astype(F32)
        q = xq * _sigmoid(xq)
        f = lb + (1.0 - lb) * _sigmoid(f_raw.astype(F32))
        lf = jnp.log(f)
        hi = lf.astype(BF16)
        mid = (lf - hi.astype(F32)).astype(BF16)
        cc = jnp.dot(tri2[d], hi, preferred_element_type=F32) + jnp.dot(tri2[d], mid, preferred_element_type=F32)
        stage1.append((q, 1.0 - f, v, cc[:C], cc[C:]))
    stage2 = []
    for d, (q, k, v, cum, cum_loc) in enumerate(stage1):
        ref_pt = cum - cum_loc
        edge = cum[0:1, :] if d == 1 else cum[C - 1:C, :]
        q_inter = (q * jnp.exp(cum)).astype(BF16)
        q_loc = (q * jnp.exp(cum_loc)).astype(BF16)
        k_st = (k * jnp.exp(edge - cum)).astype(BF16)
        k_sub = [(k * jnp.exp(jnp.minimum(ref_pt[i * S:i * S + 1, :] - cum, EXP_CLAMP))).astype(BF16)
                 for i in range(n_sub)]
        stage2.append((q_inter, q_loc, k_st, k_sub, jnp.exp(edge), v))
    chains = [(d, h) for d in range(2) for h in range(HG_HEADS)]
    hs = lambda h: slice(h * HG_D, (h + 1) * HG_D)
    st_old, o_inter, scores = {}, {}, {}
    for d, h in chains:
        q_inter, q_loc, _, k_sub, _, _ = stage2[d]
        st_old[d, h] = st_refs[d][h]
        o_inter[d, h] = lax.dot_general(q_inter[:, hs(h)], st_old[d, h].astype(BF16), nt, preferred_element_type=F32)
        sc = jnp.concatenate([lax.dot_general(q_loc[i * S:(i + 1) * S, hs(h)], k_sub[i][:, hs(h)], nt,
                                              preferred_element_type=F32) for i in range(n_sub)], axis=0)
        scores[d, h] = jnp.where(keep[d], sc, 0.0).astype(BF16)
    outs = {}
    for d, h in chains:
        outs[d, h] = o_inter[d, h] + jnp.dot(scores[d, h], stage2[d][5][:, hs(h)], preferred_element_type=F32)
    for d, h in chains:
        _, _, k_st, _, dec, v = stage2[d]
        st_refs[d][h] = st_old[d, h] * dec[:, hs(h)] + lax.dot_general(
            v[:, hs(h)], k_st[:, hs(h)], (((0,), (0,)), ((), ())), preferred_element_type=F32)
    return [jnp.concatenate([outs[d, h] for h in range(HG_HEADS)], axis=-1) for d in range(2)]


def _hgrn_kernel(lb_ref, qf_ref, ff_ref, vf_ref, qb_ref, fb_ref, vb_ref, of_ref, ob_ref, stf_ref, stb_ref, *, rows):
    @pl.when(pl.program_id(1) == 0)
    def _():
        stf_ref[...] = jnp.zeros_like(stf_ref)
        stb_ref[...] = jnp.zeros_like(stb_ref)

    C, S = HG_CHUNK, HG_SUB
    r = lax.broadcasted_iota(jnp.int32, (2 * C, C), 0)
    c = lax.broadcasted_iota(jnp.int32, (2 * C, C), 1)
    rr = jnp.where(r >= C, r - C, r)
    local = (r < C) | ((rr // S) == (c // S))
    tri2 = [jnp.where((c <= rr) & local, 1.0, 0.0).astype(BF16), jnp.where((c >= rr) & local, 1.0, 0.0).astype(BF16)]
    r1 = lax.broadcasted_iota(jnp.int32, (C, C), 0)
    c1 = lax.broadcasted_iota(jnp.int32, (C, C), 1)
    keep = [c1 <= r1, c1 >= r1]
    lb = lb_ref[...]
    n = rows // C

    def body(j, carry):
        a = pl.multiple_of(j * C, C)
        b = pl.multiple_of((n - 1 - j) * C, C)
        dirs = [(qf_ref[pl.ds(a, C), :], ff_ref[pl.ds(a, C), :], vf_ref[pl.ds(a, C), :]),
                (qb_ref[pl.ds(b, C), :], fb_ref[pl.ds(b, C), :], vb_ref[pl.ds(b, C), :])]
        o_f, o_b = _hgrn_step(dirs, lb, (stf_ref, stb_ref), tri2, keep)
        of_ref[pl.ds(a, C), :] = o_f.astype(of_ref.dtype)
        ob_ref[pl.ds(b, C), :] = o_b.astype(ob_ref.dtype)
        return carry

    lax.fori_loop(0, n, body, 0)


def _hgrn(proj3, lb, rows):
    B, S, _ = proj3.shape
    n = S // rows
    W = HG_WIDTH

    def fwd(cb):
        return pl.BlockSpec((None, rows, W), lambda b, c: (b, c, cb))

    def bwd(cb):
        return pl.BlockSpec((None, rows, W), lambda b, c: (b, n - 1 - c, cb))

    out = jax.ShapeDtypeStruct((B, S, W), BF16)
    return pl.pallas_call(
        functools.partial(_hgrn_kernel, rows=rows),
        grid=(B, n),
        in_specs=[pl.BlockSpec((1, W), lambda b, c: (0, 0)),
                  fwd(CW_HQ), fwd(CW_FF), fwd(CW_HI), bwd(CW_HQ), bwd(CW_FB), bwd(CW_HI)],
        out_specs=[pl.BlockSpec((None, rows, W), lambda b, c: (b, c, 0)),
                   pl.BlockSpec((None, rows, W), lambda b, c: (b, n - 1 - c, 0))],
        out_shape=[out, out],
        scratch_shapes=[pltpu.VMEM((HG_HEADS, HG_D, HG_D), F32), pltpu.VMEM((HG_HEADS, HG_D, HG_D), F32)],
        compiler_params=_cparams(("parallel", "arbitrary")),
        name="hgrn2",
    )(lb.reshape(1, W), proj3, proj3, proj3, proj3, proj3, proj3)


LOG2E = 1.4426950408889634
N_FEAT = 6
UNDERFLOW = 160.0


def _attn_chain(s, colterm, vt, m_ref, acc_ref, c, cols):
    m_old = m_ref[c, :, cols]
    m_new = jnp.maximum(m_old, jnp.max(s, axis=0, keepdims=True) + colterm)
    alpha = jnp.exp2(m_old - m_new)
    p = jnp.exp2(s - (m_new - colterm)).astype(BF16)
    acc_ref[c, :, cols] = alpha * acc_ref[c, :, cols] + jnp.dot(vt, p, preferred_element_type=F32)
    m_ref[c, :, cols] = m_new


_ATT_CHAINS = [(c, hf) for c in range(2) for hf in range(2)]
_NT = (((1,), (1,)), ((), ()))


def _attn_kernel(sc_ref, q_ref, k_ref, vt_ref, g_ref, o_ref, qa_ref, ka_ref, e_ref, kn_ref, s0_ref, s1_ref,
                 m_ref, acc_ref, *, t, n, out_scale):
    h, qi = pl.program_id(1), pl.program_id(2)
    slope2 = sc_ref[h] * LOG2E
    half = t // 2
    lane = lax.broadcasted_iota(jnp.int32, (t, LANES), 1)
    row = lax.broadcasted_iota(jnp.int32, (t, LANES), 0)
    feat_lane = [lane - (DA_DK if c == 0 else 0) for c in range(2)]
    is_feat = [(fl >= 0) & (fl < N_FEAT) for fl in feat_lane]
    own = [(lane < DA_DK) == (c == 0) for c in range(2)]

    @pl.when(qi == 0)
    def _():
        r_hi = ((row >> 4) << 4).astype(F32)
        r_lo = (row & 15).astype(F32)
        kn = [jnp.zeros((1, 1), F32), jnp.zeros((1, 1), F32)]
        for c in range(2):
            fk = jnp.where((feat_lane[c] & 1) == 0, r_hi, r_lo).astype(BF16)
            for j in range(n):
                kj = k_ref[j * t:(j + 1) * t, :]
                ka_ref[c, j * t:(j + 1) * t, :] = jnp.where(is_feat[c], fk, kj)
                k2 = jnp.where(own[c], kj.astype(F32), 0.0)
                kn[c] = jnp.maximum(kn[c], jnp.max(jnp.sum(k2 * k2, axis=1, keepdims=True), axis=0, keepdims=True))
        kn_ref[...] = jnp.concatenate([jnp.broadcast_to(kn[0], (1, LANES)), jnp.broadcast_to(kn[1], (1, LANES))], axis=0)
        rel = lax.broadcasted_iota(jnp.int32, (t, t), 1) - lax.broadcasted_iota(jnp.int32, (t, t), 0)
        e_ref[...] = jnp.abs(rel).astype(F32) * (-slope2)

    m_ref[...] = jnp.full_like(m_ref, -jnp.inf)
    acc_ref[...] = jnp.zeros_like(acc_ref)
    q = q_ref[...].astype(F32) * (DA_DK ** -0.5 * LOG2E)
    sv = jnp.full((t, LANES), slope2, F32)
    s_hi = sv.astype(BF16).astype(F32)
    s_mid = (sv - s_hi).astype(BF16).astype(F32)
    s_lo = (sv - s_hi - s_mid).astype(BF16).astype(F32)
    qn = []
    for c in range(2):
        fl = feat_lane[c]
        feat_q = jnp.where(is_feat[c], jnp.where(fl < 2, s_hi, jnp.where(fl < 4, s_mid, s_lo)), 0.0)
        qc = jnp.where(own[c], q, 0.0).astype(BF16)
        qa_ref[0 + c] = jnp.where(own[c], q, feat_q).astype(BF16)
        qa_ref[2 + c] = jnp.where(own[c], q, -feat_q).astype(BF16)
        qa_ref[4 + c] = qc
        q2 = qc.astype(F32)
        qn.append(jnp.max(jnp.sum(q2 * q2, axis=1, keepdims=True), axis=0, keepdims=True))

    near = jnp.where(qi > 0, qi - 1, 0)
    n_off = n - 1

    def tile_of(u, lo):
        o = jnp.where(u == 1, near, lo + (u - 2) + (lo + (u - 2) >= near).astype(jnp.int32))
        return jnp.where(u == 0, qi, o + (o >= qi).astype(jnp.int32))

    def scores_to(slot, u, lo):
        j = tile_of(u, lo)
        qsel = jnp.where(j == qi, 4, jnp.where(j < qi, 0, 2))
        r0 = pl.multiple_of(j * t, t)
        for c, hf in _ATT_CHAINS:
            qa = qa_ref[qsel + c, hf * half:(hf + 1) * half, :]
            slot[c, :, hf * half:(hf + 1) * half] = lax.dot_general(ka_ref[c, pl.ds(r0, t), :], qa, _NT,
                                                                      preferred_element_type=F32)

    def consume(slot, u, lo, diagonal=False):
        j = tile_of(u, lo)
        vt = vt_ref[:, pl.ds(pl.multiple_of(j * t, t), t)]
        if diagonal:
            colterm = jnp.zeros((1, t), F32)
        else:
            sign = jnp.where(j < qi, -slope2, slope2)
            col = lax.broadcasted_iota(jnp.int32, (1, t), 1) + (qi - j) * t
            colterm = col.astype(F32) * sign
        for c, hf in _ATT_CHAINS:
            cols = slice(hf * half, (hf + 1) * half)
            s = slot[c, :, cols]
            if diagonal:
                s = s + e_ref[:, cols]
            _attn_chain(s, colterm[:, cols], vt, m_ref, acc_ref, c, cols)

    i32 = jnp.int32
    zero = i32(0)
    scores_to(s0_ref, i32(0), zero)
    if n > 1:
        scores_to(s1_ref, i32(1), zero)
    consume(s0_ref, i32(0), zero, diagonal=True)
    if n > 1:
        reach = jnp.zeros((1, 1), F32)
        for c in range(2):
            m_min = jnp.min(m_ref[c], axis=1, keepdims=True)
            x = 0.5 * (qn[c] + kn_ref[c:c + 1, 0:1]) - m_min + UNDERFLOW
            reach = jnp.maximum(reach, x)
        d_f = jnp.floor(reach / (slope2 * t) - (1.0 / t)) + 1.0
        d_max = jnp.clip(d_f, 0.0, float(n)).astype(jnp.int32)[0, 0]
        lo = jnp.maximum(qi - d_max, 0)
        hi = jnp.minimum(qi + d_max - 1, n_off - 1)
        count = jnp.maximum(hi - lo + 1, 0)
        trips = jnp.maximum(count - 1, 0) // 2

        def pair(p, carry):
            u = 1 + 2 * p
            scores_to(s0_ref, u + 1, lo)
            consume(s1_ref, u, lo)
            scores_to(s1_ref, u + 2, lo)
            consume(s0_ref, u + 1, lo)
            return carry

        lax.fori_loop(0, trips, pair, 0)
        rem = count - 2 * trips

        @pl.when(rem == 1)
        def _():
            consume(s1_ref, count, lo)

        @pl.when(rem == 2)
        def _():
            scores_to(s0_ref, count, lo)
            consume(s1_ref, count - 1, lo)
            consume(s0_ref, count, lo)

    lam = sc_ref[DA_HEADS]
    a0, a1 = acc_ref[0], acc_ref[1]
    o = a0[:DA_DV] / a0[DA_DV:DA_DV + 1] - lam * (a1[:DA_DV] / a1[DA_DV:DA_DV + 1])
    o = o * lax.rsqrt(jnp.mean(o * o, axis=0, keepdims=True) + EPS) * (g_ref[...] * out_scale)
    o_ref[...] = o.T.astype(o_ref.dtype)


def _diff_attn(proj3, vt, scalars, g_da_col, t, out_scale):
    B, S, _ = proj3.shape
    n = S // t
    return pl.pallas_call(
        functools.partial(_attn_kernel, t=t, n=n, out_scale=out_scale),
        grid_spec=pltpu.PrefetchScalarGridSpec(
            num_scalar_prefetch=1,
            grid=(B, DA_HEADS, n),
            in_specs=[pl.BlockSpec((None, t, LANES), lambda b, h, i, sc: (b, i, CB_DQ + h)),
                      pl.BlockSpec((None, S, LANES), lambda b, h, i, sc: (b, 0, CB_DK + h)),
                      pl.BlockSpec((VT_ROWS, S), lambda b, h, i, sc: (h, b)),
                      pl.BlockSpec((DA_DV, 1), lambda b, h, i, sc: (0, 0))],
            out_specs=pl.BlockSpec((None, t, LANES), lambda b, h, i, sc: (b, i, h)),
            scratch_shapes=[pltpu.VMEM((6, t, LANES), BF16), pltpu.VMEM((2, S, LANES), BF16),
                            pltpu.VMEM((t, t), F32), pltpu.VMEM((2, LANES), F32),
                            pltpu.VMEM((2, t, t), F32), pltpu.VMEM((2, t, t), F32),
                            pltpu.VMEM((2, 1, t), F32), pltpu.VMEM((2, VT_ROWS, t), F32)]),
        out_shape=jax.ShapeDtypeStruct((B, S, DA_WIDTH), BF16),
        compiler_params=_cparams(("parallel", "parallel", "arbitrary")),
        name="diff_attn",
    )(scalars, proj3, proj3, vt, g_da_col)


def _memkv_kernel(m_ref, g_ref, w_ref, o_ref):
    h = _rms(m_ref[...], g_ref[...]).astype(BF16)
    o_ref[...] = jnp.dot(h, w_ref[...], preferred_element_type=F32).astype(BF16)


def _memkv(mem2d, g_mem, w_ckv_bf):
    T = mem2d.shape[0]
    return pl.pallas_call(
        _memkv_kernel,
        grid=(T // N_MEM,),
        in_specs=[pl.BlockSpec((N_MEM, D_MODEL), lambda i: (i, 0)),
                  pl.BlockSpec((1, D_MODEL), lambda i: (0, 0)),
                  pl.BlockSpec((D_MODEL, 2 * D_MODEL), lambda i: (0, 0))],
        out_specs=pl.BlockSpec((N_MEM, 2 * D_MODEL), lambda i: (i, 0)),
        out_shape=jax.ShapeDtypeStruct((T, 2 * D_MODEL), BF16),
        compiler_params=_cparams(("parallel",)),
        name="mem_kv",
    )(mem2d, g_mem, w_ckv_bf)


def _post_kernel(x_ref, of_ref, ob_ref, hgate_ref, oda_ref, kv_ref, ghg_ref, wout_ref, gca_ref, wcq_ref, wco_ref,
                 gffn_ref, wr_ref, br_ref,
                 x2_ref, hn_ref, ri_ref, rw_ref, cnt_ref, carry_ref, *, tm):
    @pl.when(pl.program_id(0) == 0)
    def _():
        carry_ref[...] = jnp.zeros_like(carry_ref)

    hm = tm // 2
    halves = [slice(0, hm), slice(hm, tm)]
    f32dot = functools.partial(jnp.dot, preferred_element_type=F32)

    ghg = ghg_ref[...]
    mix_hg = []
    for r in halves:
        o_hg = of_ref[r, :].astype(F32) + ob_ref[r, :].astype(F32)
        parts = [_rms(o_hg[:, i * HG_D:(i + 1) * HG_D], ghg[:, i * HG_D:(i + 1) * HG_D]) for i in range(HG_HEADS)]
        mix_hg.append((jnp.concatenate(parts, axis=-1) * _silu(hgate_ref[r, :].astype(F32))).astype(BF16))
    x1 = [x_ref[r, :] + f32dot(mh, wout_ref[:HG_WIDTH, :]) + f32dot(oda_ref[r, :], wout_ref[HG_WIDTH:, :])
          for r, mh in zip(halves, mix_hg)]

    hq_in = [_rms(v, gca_ref[...]).astype(BF16) for v in x1]
    hq = [f32dot(v, wcq_ref[...]).astype(BF16) for v in hq_in]
    scores = [[lax.dot_general(v[:, i * CA_DH:(i + 1) * CA_DH], kv_ref[:, i * CA_DH:(i + 1) * CA_DH], _NT,
                               preferred_element_type=F32) * (CA_DH ** -0.5) for i in range(CA_HEADS)] for v in hq]
    probs = []
    for sc in scores:
        ps = []
        for s_ in sc:
            e = jnp.exp(s_ - jnp.max(s_, axis=-1, keepdims=True))
            ps.append((e / jnp.sum(e, axis=-1, keepdims=True)).astype(BF16))
        probs.append(ps)
    ca = [jnp.concatenate([f32dot(ps[i], kv_ref[:, D_MODEL + i * CA_DH:D_MODEL + (i + 1) * CA_DH]).astype(BF16)
                           for i in range(CA_HEADS)], axis=-1) for ps in probs]
    x2 = [v + f32dot(c_, wco_ref[...]) for v, c_ in zip(x1, ca)]

    hn = [_rms(v, gffn_ref[...]) for v in x2]
    for r, v2, vn in zip(halves, x2, hn):
        x2_ref[r, :] = v2
        hn_ref[r, :] = _pack_bf16_pairs(vn)
    logits = []
    for vn in hn:
        hi = vn.astype(BF16)
        lo = (vn - hi.astype(F32)).astype(BF16)
        logits.append(f32dot(hi, wr_ref[0]) + f32dot(lo, wr_ref[0]) + f32dot(hi, wr_ref[1]) + br_ref[...])
    lane = lax.broadcasted_iota(jnp.int32, (hm, LANES), 1)
    big = jnp.int32(1 << 20)
    neg = jnp.float32(-jnp.inf)

    def first_max(vals):
        m = jnp.max(vals, axis=-1, keepdims=True)
        return m, jnp.min(jnp.where(vals == m, lane, big), axis=-1, keepdims=True)

    routed = []
    for lg in logits:
        gl = jnp.where(lane < N_GROUPS, lg, neg)
        gmax, gidx = first_max(gl)
        g_val = 1.0 / jnp.sum(jnp.exp(gl - gmax), axis=-1, keepdims=True)
        e_lane = lane - N_GROUPS
        in_grp = (e_lane >= 0) & (e_lane < N_EXPERTS) & ((e_lane // EXP_PER_GROUP) == gidx)
        el = jnp.where(in_grp, lg, neg)
        m1, i1 = first_max(el)
        m2, i2 = first_max(jnp.where(lane == i1, neg, el))
        t = jnp.exp(m2 - m1)
        routed.append((i1, i2, g_val / (1.0 + t), g_val * t / (1.0 + t)))

    r_i = lax.broadcasted_iota(jnp.int32, (hm, hm), 0)
    c_i = lax.broadcasted_iota(jnp.int32, (hm, hm), 1)
    prefix = (c_i < r_i).astype(BF16)
    carry = carry_ref[...]
    for r, (i1, i2, w1, w2) in zip(halves, routed):
        hit1, hit2 = lane == i1, lane == i2
        onehot = (hit1 | hit2).astype(BF16)
        before = f32dot(prefix, onehot) + carry
        r1 = jnp.sum(jnp.where(hit1, before, 0.0), axis=-1, keepdims=True).astype(jnp.int32)
        r2 = jnp.sum(jnp.where(hit2, before, 0.0), axis=-1, keepdims=True).astype(jnp.int32)
        carry = carry + jnp.sum(onehot.astype(F32), axis=0, keepdims=True)
        ri_ref[r, :] = jnp.where(lane == 0, i1 - N_GROUPS, jnp.where(lane == 1, i2 - N_GROUPS,
                                 jnp.where(lane == 2, r1, jnp.where(lane == 3, r2, 0))))
        rw_ref[r, :] = jnp.where(lane == 0, w1, jnp.where(lane == 1, w2, 0.0))
    carry_ref[...] = carry
    cnt_ref[...] = carry


def _post(x2d, o_f, o_b, proj, o_da, kv3, S, w, tm):
    T = x2d.shape[0]
    per_b = S // tm
    row = lambda i: (i, 0)
    const = lambda i: (0, 0)
    return pl.pallas_call(
        functools.partial(_post_kernel, tm=tm),
        grid=(T // tm,),
        in_specs=[pl.BlockSpec((tm, D_MODEL), row),
                  pl.BlockSpec((tm, HG_WIDTH), row),
                  pl.BlockSpec((tm, HG_WIDTH), row),
                  pl.BlockSpec((tm, HG_WIDTH), lambda i: (i, CW_HGATE)),
                  pl.BlockSpec((tm, DA_WIDTH), row),
                  pl.BlockSpec((None, N_MEM, 2 * D_MODEL), lambda i: (i // per_b, 0, 0)),
                  pl.BlockSpec((1, HG_WIDTH), const),
                  pl.BlockSpec((D_MODEL, D_MODEL), const),
                  pl.BlockSpec((1, D_MODEL), const),
                  pl.BlockSpec((D_MODEL, D_MODEL), const),
                  pl.BlockSpec((D_MODEL, D_MODEL), const),
                  pl.BlockSpec((1, D_MODEL), const),
                  pl.BlockSpec((2, D_MODEL, LANES), lambda i: (0, 0, 0)),
                  pl.BlockSpec((1, LANES), const)],
        out_specs=[pl.BlockSpec((tm, D_MODEL), row),
                   pl.BlockSpec((tm, D_PACK), row),
                   pl.BlockSpec((tm, LANES), row),
                   pl.BlockSpec((tm, LANES), row),
                   pl.BlockSpec((1, LANES), const)],
        out_shape=[jax.ShapeDtypeStruct((T, D_MODEL), F32),
                   jax.ShapeDtypeStruct((T, D_PACK), jnp.uint32),
                   jax.ShapeDtypeStruct((T, LANES), jnp.int32),
                   jax.ShapeDtypeStruct((T, LANES), F32),
                   jax.ShapeDtypeStruct((1, LANES), F32)],
        scratch_shapes=[pltpu.VMEM((1, LANES), F32)],
        compiler_params=_cparams(("arbitrary",)),
        name="post_mixer",
    )(x2d, o_f, o_b, proj, o_da, kv3, w["g_hg"], w["w_out"], w["g_ca"], w["w_cq"], w["w_co"], w["g_ffn"],
      w["w_r"], w["b_r"])


def _expert_kernel(be_ref, nu_ref, src_ref, src_next_ref, src_prev_ref, hn_ref, wg_ref, wu_ref, wd_ref, yt_ref,
                   xbuf_ref, ybuf_ref, gsem, ssem, *, n_tok):
    del be_ref
    i = pl.program_id(0)
    n_used = nu_ref[0]
    slot = i % 2

    def gather(src, buf_slot, r):
        tok = jnp.minimum(src[0, r] >> 1, n_tok - 1)
        return pltpu.make_async_copy(hn_ref.at[pl.ds(tok, 1), :], xbuf_ref.at[buf_slot, pl.ds(r, 1), :],
                                     gsem.at[buf_slot])

    def scatter(src, r):
        return pltpu.make_async_copy(ybuf_ref.at[pl.ds(r, 1), :], yt_ref.at[pl.ds(src[0, r], 1), :], ssem.at[0])

    def zero_fill(k):
        return pltpu.make_async_copy(ybuf_ref, yt_ref.at[pl.ds(2 * n_tok + k * MOE_BLOCK, MOE_BLOCK), :], ssem.at[0])

    def for_rows(fn):
        def body(r, c):
            fn(r)
            return c
        lax.fori_loop(0, MOE_BLOCK, body, 0, unroll=8)

    @pl.when((i == 0) & (n_used > 0))
    def _():
        ybuf_ref[...] = jnp.zeros_like(ybuf_ref)
        n_fill = (yt_ref.shape[0] - 2 * n_tok) // MOE_BLOCK
        for k in range(n_fill):
            zero_fill(k).start()
        for k in range(n_fill):
            zero_fill(k).wait()
        for_rows(lambda r: gather(src_ref, 0, r).start())

    @pl.when(i < n_used)
    def _():
        for_rows(lambda r: gather(src_ref, slot, r).wait())
        for r in range(MOE_BLOCK):
            gather(src_next_ref, 1 - slot, r).start()
            scatter(src_prev_ref, r).start()
        xb = _unpack_bf16_pairs(xbuf_ref[slot]).astype(BF16)
        a = jnp.dot(xb, wg_ref[...], preferred_element_type=F32)
        u = jnp.dot(xb, wu_ref[...], preferred_element_type=F32)
        hmid = (_silu(a) * u).astype(BF16)
        y = _pack_bf16_pairs(jnp.dot(hmid, wd_ref[...], preferred_element_type=F32))
        for_rows(lambda r: scatter(src_prev_ref, r).wait())
        ybuf_ref[...] = y

        @pl.when(i == n_used - 1)
        def _():
            for_rows(lambda r: scatter(src_ref, r).start())
            for_rows(lambda r: scatter(src_ref, r).wait())
            for_rows(lambda r: gather(src_next_ref, 1 - slot, r).wait())


def _experts(hn, slot_src, block_expert, n_used, w_gate, w_up, w_down):
    n_tok = hn.shape[0]
    n_slots = slot_src.shape[0]
    nb = n_slots // MOE_BLOCK
    lead = 2 * n_tok + jnp.arange(MOE_BLOCK, dtype=jnp.int32)
    src3 = jnp.concatenate([lead, slot_src]).reshape(nb + 1, 1, MOE_BLOCK)
    n_rows = n_slots + MOE_BLOCK

    def src_spec(off):
        return pl.BlockSpec((None, 1, MOE_BLOCK), lambda i, be, nu: (jnp.minimum(i + 1 + off, nb), 0, 0),
                            memory_space=pltpu.SMEM)

    return pl.pallas_call(
        functools.partial(_expert_kernel, n_tok=n_tok),
        grid_spec=pltpu.PrefetchScalarGridSpec(
            num_scalar_prefetch=2,
            grid=(nb,),
            in_specs=[src_spec(0), src_spec(1), src_spec(-1),
                      pl.BlockSpec(memory_space=pl.ANY),
                      pl.BlockSpec((None, D_MODEL, D_EXPERT), lambda i, be, nu: (be[i], 0, 0)),
                      pl.BlockSpec((None, D_MODEL, D_EXPERT), lambda i, be, nu: (be[i], 0, 0)),
                      pl.BlockSpec((None, D_EXPERT, D_MODEL), lambda i, be, nu: (be[i], 0, 0))],
            out_specs=pl.BlockSpec(memory_space=pl.ANY),
            scratch_shapes=[pltpu.VMEM((2, MOE_BLOCK, D_PACK), jnp.uint32), pltpu.VMEM((MOE_BLOCK, D_PACK), jnp.uint32),
                            pltpu.SemaphoreType.DMA((2,)), pltpu.SemaphoreType.DMA((1,))]),
        out_shape=jax.ShapeDtypeStruct((n_rows, D_PACK), jnp.uint32),
        compiler_params=_cparams(("arbitrary",), has_side_effects=True),
        name="moe_experts",
    )(block_expert, n_used, src3, src3, src3, hn, w_gate, w_up, w_down)


def _combine_kernel(x2_ref, rw_ref, g_ref, yt_ref, y_ref):
    rw = rw_ref[...]
    lane = lax.broadcasted_iota(jnp.int32, rw.shape, 1)
    w1 = jnp.sum(jnp.where(lane == 0, rw, 0.0), axis=-1, keepdims=True)
    w2 = jnp.sum(jnp.where(lane == 1, rw, 0.0), axis=-1, keepdims=True)
    x3 = x2_ref[...] + (_unpack_bf16_pairs(yt_ref[:, :D_PACK]) * w1 + _unpack_bf16_pairs(yt_ref[:, D_PACK:]) * w2)
    y_ref[...] = _rms(x3, g_ref[...])


def _combine(x2, rw, g_final, yt, tm):
    T = x2.shape[0]
    return pl.pallas_call(
        _combine_kernel,
        grid=(T // tm,),
        in_specs=[pl.BlockSpec((tm, D_MODEL), lambda i: (i, 0)),
                  pl.BlockSpec((tm, LANES), lambda i: (i, 0)),
                  pl.BlockSpec((1, D_MODEL), lambda i: (0, 0)),
                  pl.BlockSpec((tm, 2 * D_PACK), lambda i: (i, 0))],
        out_specs=pl.BlockSpec((tm, D_MODEL), lambda i: (i, 0)),
        out_shape=jax.ShapeDtypeStruct((T, D_MODEL), F32),
        compiler_params=_cparams(("parallel",)),
        name="moe_combine",
    )(x2, rw, g_final, yt)


def _tile(n, pref):
    return min(n, pref)


def _trunk(x, mem, w):
    B, S, _ = x.shape
    T = B * S
    x2d = x.reshape(T, D_MODEL)
    proj, vt = _inproj(x2d, w["g_mix"], w["w_in"], w["w_vt"], _tile(T, PROJ_TM))
    proj3 = proj.reshape(B, S, IN_COLS)
    o_f, o_b = _hgrn(proj3, w["lb"], _tile(S, HG_ROWS))
    o_da = _diff_attn(proj3, vt, w["attn_scalars"], w["g_da"], _tile(S, ATT_T), w["da_out_scale"])
    kv = _memkv(mem.reshape(B * N_MEM, D_MODEL), w["g_mem"], w["w_ckv"]).reshape(B, N_MEM, 2 * D_MODEL)
    tm = _tile(S, POST_TM)
    x2, hn, ri, rw, cnt = _post(x2d, o_f.reshape(T, HG_WIDTH), o_b.reshape(T, HG_WIDTH), proj,
                                o_da.reshape(T, DA_WIDTH), kv, S, w, tm)

    counts = cnt[0, N_GROUPS:N_GROUPS + N_EXPERTS].astype(jnp.int32)
    padded = (counts + MOE_BLOCK - 1) // MOE_BLOCK * MOE_BLOCK
    pad_end = jnp.cumsum(padded)
    pad_start = pad_end - padded
    dest = pad_start[ri[:, 0:2]] + ri[:, 2:4]
    n_slots = 2 * T + N_EXPERTS * MOE_BLOCK
    nb = n_slots // MOE_BLOCK
    block_start = jnp.arange(nb, dtype=jnp.int32) * MOE_BLOCK
    block_expert = jnp.minimum(jnp.sum((pad_end[None, :] <= block_start[:, None]).astype(jnp.int32), axis=1),
                               N_EXPERTS - 1)
    n_used = (pad_end[-1:] // MOE_BLOCK).astype(jnp.int32)
    e_slot = jnp.repeat(block_expert, MOE_BLOCK)
    pad_before = jnp.cumsum(padded - counts) - (padded - counts)
    pad_rank = pad_before[e_slot] + jnp.arange(n_slots, dtype=jnp.int32) - pad_start[e_slot] - counts[e_slot]
    slot_src = (2 * T + MOE_BLOCK + pad_rank).at[dest.reshape(-1)].set(jnp.arange(2 * T, dtype=jnp.int32))
    yt = _experts(hn, slot_src, block_expert, n_used, w["w_gate"], w["w_up"], w["w_down"])
    y = _combine(x2, rw, w["g_final"], yt.reshape(-1, 2 * D_PACK), tm)
    return y.reshape(B, S, D_MODEL)


def _split_bf16(a):
    hi = a.astype(BF16)
    return jnp.stack([hi, (a - hi.astype(F32)).astype(BF16)])


def kernel(x_prompt, x_sample, mem_prompt, mem_sample, g_mix, w_in, hg_lb, g_hg, lam_q1, lam_k1, lam_q2, lam_k2,
           g_da, w_out, g_ca, g_mem, w_cq, w_ckv, w_co, g_ffn, w_rg, b_rg, w_re, b_re, w_gate, w_up, w_down, g_final):
    l = 0
    lam_init = 0.8 - 0.6 * math.exp(-0.3 * l)
    lam = (jnp.exp(jnp.sum(lam_q1[l] * lam_k1[l])) - jnp.exp(jnp.sum(lam_q2[l] * lam_k2[l])) + lam_init)
    slopes = jnp.power(2.0, -8.0 * jnp.arange(1, DA_HEADS + 1, dtype=F32) / DA_HEADS)
    pad = LANES - N_GROUPS - N_EXPERTS
    w = {
        "g_mix": g_mix[l][None], "w_in": w_in[l].astype(BF16),
        "w_vt": w_in[l][:, CB_DV * LANES:].T.astype(BF16),
        "lb": jnp.cumsum(jax.nn.softmax(hg_lb, axis=0), axis=0)[l],
        "g_hg": g_hg[l][None],
        "attn_scalars": jnp.concatenate([slopes, lam[None]]).astype(F32),
        "g_da": g_da[l][:, None], "da_out_scale": 1.0 - lam_init,
        "w_out": w_out[l].astype(BF16), "g_ca": g_ca[l][None], "g_mem": g_mem[l][None],
        "w_cq": w_cq[l].astype(BF16), "w_ckv": w_ckv[l].astype(BF16), "w_co": w_co[l].astype(BF16),
        "g_ffn": g_ffn[l][None],
        "w_r": _split_bf16(jnp.pad(jnp.concatenate([w_rg[l], w_re[l]], axis=1), ((0, 0), (0, pad)))),
        "b_r": jnp.pad(jnp.concatenate([b_rg[l], b_re[l]]), (0, pad))[None],
        "w_gate": w_gate[l].astype(BF16), "w_up": w_up[l].astype(BF16), "w_down": w_down[l].astype(BF16),
        "g_final": g_final[None],
    }
    return (_trunk(x_prompt, mem_prompt, w), _trunk(x_sample, mem_sample, w))
```

```python
import functools
import math

import jax
import jax.numpy as jnp
from jax import lax
from jax.experimental import pallas as pl
from jax.experimental.pallas import tpu as pltpu

F32 = jnp.float32
BF16 = jnp.bfloat16

D_MODEL = 1024
N_MEM = 256
EPS = 1e-6
HG_HEADS = 4
HG_D = 128
HG_WIDTH = HG_HEADS * HG_D
DA_HEADS = 4
DA_DK = 64
DA_DV = 128
DA_WIDTH = DA_HEADS * DA_DV
IN_COLS = 4096
CA_HEADS = 4
CA_DH = 256
N_GROUPS = 4
EXP_PER_GROUP = 8
N_EXPERTS = 32
D_EXPERT = 512
LANES = 128
D_PACK = D_MODEL // 2
VT_ROWS = DA_DV + 16

CW_HQ, CW_FF, CW_FB, CW_HI, CW_HGATE = 0, 1, 2, 3, 4
CB_DQ, CB_DK, CB_DV = 20, 24, 28

HG_CHUNK = 64
HG_SUB = 16
EXP_CLAMP = 80.0
MOE_BLOCK = 256

VMEM_LIMIT = 48 * 1024 * 1024

PROJ_TM = 512
HG_ROWS = 256
ATT_T = 512
POST_TM = 512


def _cparams(sem, **kw):
    return pltpu.CompilerParams(dimension_semantics=sem, vmem_limit_bytes=VMEM_LIMIT, **kw)


def _rms(x, g):
    return x * lax.rsqrt(jnp.mean(x * x, axis=-1, keepdims=True) + EPS) * g


def _silu(x):
    return x * (1.0 / (1.0 + jnp.exp(-x)))


def _pack_bf16_pairs(x):
    n = x.shape[1] // 2
    lo = lax.bitcast_convert_type(x[:, :n].astype(BF16).astype(F32), jnp.uint32)
    hi = lax.bitcast_convert_type(x[:, n:].astype(BF16).astype(F32), jnp.uint32)
    return (lo >> 16) | (hi & jnp.uint32(0xFFFF0000))


def _unpack_bf16_pairs(w):
    lo = lax.bitcast_convert_type(w << 16, F32)
    hi = lax.bitcast_convert_type(w & jnp.uint32(0xFFFF0000), F32)
    return jnp.concatenate([lo, hi], axis=1)


def _inproj_kernel(x_ref, g_ref, w_ref, wvt_ref, o_ref, vt_ref, *, n_chunk):
    h = _rms(x_ref[...], g_ref[...]).astype(BF16)
    for j in range(IN_COLS // n_chunk):
        sl = slice(j * n_chunk, (j + 1) * n_chunk)
        o_ref[:, sl] = jnp.dot(h, w_ref[:, sl], preferred_element_type=F32).astype(BF16)
    vt = lax.dot_general(wvt_ref[...], h, (((1,), (1,)), ((), ())), preferred_element_type=F32).astype(BF16)
    for hd in range(DA_HEADS):
        vt_ref[hd * VT_ROWS:hd * VT_ROWS + DA_DV, :] = vt[hd * DA_DV:(hd + 1) * DA_DV]
        vt_ref[hd * VT_ROWS + DA_DV:(hd + 1) * VT_ROWS, :] = jnp.ones((VT_ROWS - DA_DV, vt.shape[1]), BF16)


def _inproj(x2d, g_mix, w_in_bf, w_vt_bf, tm):
    T = x2d.shape[0]
    return pl.pallas_call(
        functools.partial(_inproj_kernel, n_chunk=512),
        grid=(T // tm,),
        in_specs=[pl.BlockSpec((tm, D_MODEL), lambda i: (i, 0)),
                  pl.BlockSpec((1, D_MODEL), lambda i: (0, 0)),
                  pl.BlockSpec((D_MODEL, IN_COLS), lambda i: (0, 0)),
                  pl.BlockSpec((DA_WIDTH, D_MODEL), lambda i: (0, 0))],
        out_specs=[pl.BlockSpec((tm, IN_COLS), lambda i: (i, 0)),
                   pl.BlockSpec((DA_HEADS * VT_ROWS, tm), lambda i: (0, i))],
        out_shape=[jax.ShapeDtypeStruct((T, IN_COLS), BF16), jax.ShapeDtypeStruct((DA_HEADS * VT_ROWS, T), BF16)],
        compiler_params=_cparams(("parallel",)),
        name="inproj",
    )(x2d, g_mix, w_in_bf, w_vt_bf)


def _sigmoid(x):
    return 0.5 + 0.5 * jnp.tanh(0.5 * x)


def _hgrn_step(dirs, lb, st_refs, tri2, keep):
    C, S = HG_CHUNK, HG_SUB
    n_sub = C // S
    nt = (((1,), (1,)), ((), ()))
    stage1 = []
    for d, (q_raw, f_raw, v) in enumerate(dirs):
        xq = q_raw.astype(F32)
        q = xq * _sigmoid(xq)
        f = lb + (1.0 - lb) * _sigmoid(f_raw.astype(F32))
        lf = jnp.log(f)
        hi = lf.astype(BF16)
        mid = (lf - hi.astype(F32)).astype(BF16)
        cc = jnp.dot(tri2[d], hi, preferred_element_type=F32) + jnp.dot(tri2[d], mid, preferred_element_type=F32)
        stage1.append((q, 1.0 - f, v, cc[:C], cc[C:]))
    stage2 = []
    for d, (q, k, v, cum, cum_loc) in enumerate(stage1):
        ref_pt = cum - cum_loc
        edge = cum[0:1, :] if d == 1 else cum[C - 1:C, :]
        q_inter = (q * jnp.exp(cum)).astype(BF16)
        q_loc = (q * jnp.exp(cum_loc)).astype(BF16)
        k_st = (k * jnp.exp(edge - cum)).astype(BF16)
        k_sub = [(k * jnp.exp(jnp.minimum(ref_pt[i * S:i * S + 1, :] - cum, EXP_CLAMP))).astype(BF16)
                 for i in range(n_sub)]
        stage2.append((q_inter, q_loc, k_st, k_sub, jnp.exp(edge), v))
    chains = [(d, h) for d in range(2) for h in range(HG_HEADS)]
    hs = lambda h: slice(h * HG_D, (h + 1) * HG_D)
    st_old, o_inter, scores = {}, {}, {}
    for d, h in chains:
        q_inter, q_loc, _, k_sub, _, _ = stage2[d]
        st_old[d, h] = st_refs[d][h]
        o_inter[d, h] = lax.dot_general(q_inter[:, hs(h)], st_old[d, h].astype(BF16), nt, preferred_element_type=F32)
        sc = jnp.concatenate([lax.dot_general(q_loc[i * S:(i + 1) * S, hs(h)], k_sub[i][:, hs(h)], nt,
                                              preferred_element_type=F32) for i in range(n_sub)], axis=0)
        scores[d, h] = jnp.where(keep[d], sc, 0.0).astype(BF16)
    outs = {}
    for d, h in chains:
        outs[d, h] = o_inter[d, h] + jnp.dot(scores[d, h], stage2[d][5][:, hs(h)], preferred_element_type=F32)
    for d, h in chains:
        _, _, k_st, _, dec, v = stage2[d]
        st_refs[d][h] = st_old[d, h] * dec[:, hs(h)] + lax.dot_general(
            v[:, hs(h)], k_st[:, hs(h)], (((0,), (0,)), ((), ())), preferred_element_type=F32)
    return [jnp.concatenate([outs[d, h] for h in range(HG_HEADS)], axis=-1) for d in range(2)]


def _hgrn_kernel(lb_ref, qf_ref, ff_ref, vf_ref, qb_ref, fb_ref, vb_ref, of_ref, ob_ref, stf_ref, stb_ref, *, rows):
    @pl.when(pl.program_id(1) == 0)
    def _():
        stf_ref[...] = jnp.zeros_like(stf_ref)
        stb_ref[...] = jnp.zeros_like(stb_ref)

    C, S = HG_CHUNK, HG_SUB
    r = lax.broadcasted_iota(jnp.int32, (2 * C, C), 0)
    c = lax.broadcasted_iota(jnp.int32, (2 * C, C), 1)
    rr = jnp.where(r >= C, r - C, r)
    local = (r < C) | ((rr // S) == (c // S))
    tri2 = [jnp.where((c <= rr) & local, 1.0, 0.0).astype(BF16), jnp.where((c >= rr) & local, 1.0, 0.0).astype(BF16)]
    r1 = lax.broadcasted_iota(jnp.int32, (C, C), 0)
    c1 = lax.broadcasted_iota(jnp.int32, (C, C), 1)
    keep = [c1 <= r1, c1 >= r1]
    lb = lb_ref[...]
    n = rows // C

    def body(j, carry):
        a = pl.multiple_of(j * C, C)
        b = pl.multiple_of((n - 1 - j) * C, C)
        dirs = [(qf_ref[pl.ds(a, C), :], ff_ref[pl.ds(a, C), :], vf_ref[pl.ds(a, C), :]),
                (qb_ref[pl.ds(b, C), :], fb_ref[pl.ds(b, C), :], vb_ref[pl.ds(b, C), :])]
        o_f, o_b = _hgrn_step(dirs, lb, (stf_ref, stb_ref), tri2, keep)
        of_ref[pl.ds(a, C), :] = o_f.astype(of_ref.dtype)
        ob_ref[pl.ds(b, C), :] = o_b.astype(ob_ref.dtype)
        return carry

    lax.fori_loop(0, n, body, 0)


def _hgrn(proj3, lb, rows):
    B, S, _ = proj3.shape
    n = S // rows
    W = HG_WIDTH

    def fwd(cb):
        return pl.BlockSpec((None, rows, W), lambda b, c: (b, c, cb))

    def bwd(cb):
        return pl.BlockSpec((None, rows, W), lambda b, c: (b, n - 1 - c, cb))

    out = jax.ShapeDtypeStruct((B, S, W), BF16)
    return pl.pallas_call(
        functools.partial(_hgrn_kernel, rows=rows),
        grid=(B, n),
        in_specs=[pl.BlockSpec((1, W), lambda b, c: (0, 0)),
                  fwd(CW_HQ), fwd(CW_FF), fwd(CW_HI), bwd(CW_HQ), bwd(CW_FB), bwd(CW_HI)],
        out_specs=[pl.BlockSpec((None, rows, W), lambda b, c: (b, c, 0)),
                   pl.BlockSpec((None, rows, W), lambda b, c: (b, n - 1 - c, 0))],
        out_shape=[out, out],
        scratch_shapes=[pltpu.VMEM((HG_HEADS, HG_D, HG_D), F32), pltpu.VMEM((HG_HEADS, HG_D, HG_D), F32)],
        compiler_params=_cparams(("parallel", "arbitrary")),
        name="hgrn2",
    )(lb.reshape(1, W), proj3, proj3, proj3, proj3, proj3, proj3)


LOG2E = 1.4426950408889634
N_FEAT = 6
UNDERFLOW = 160.0


def _attn_chain(s, colterm, vt, m_ref, acc_ref, c, cols):
    m_old = m_ref[c, :, cols]
    m_new = jnp.maximum(m_old, jnp.max(s, axis=0, keepdims=True) + colterm)
    alpha = jnp.exp2(m_old - m_new)
    p = jnp.exp2(s - (m_new - colterm)).astype(BF16)
    acc_ref[c, :, cols] = alpha * acc_ref[c, :, cols] + jnp.dot(vt, p, preferred_element_type=F32)
    m_ref[c, :, cols] = m_new


_ATT_CHAINS = [(c, hf) for c in range(2) for hf in range(2)]
_NT = (((1,), (1,)), ((), ()))


def _attn_kernel(sc_ref, q_ref, k_ref, vt_ref, g_ref, o_ref, qa_ref, ka_ref, e_ref, kn_ref, s0_ref, s1_ref,
                 m_ref, acc_ref, *, t, n, out_scale):
    h, qi = pl.program_id(1), pl.program_id(2)
    slope2 = sc_ref[h] * LOG2E
    half = t // 2
    lane = lax.broadcasted_iota(jnp.int32, (t, LANES), 1)
    row = lax.broadcasted_iota(jnp.int32, (t, LANES), 0)
    feat_lane = [lane - (DA_DK if c == 0 else 0) for c in range(2)]
    is_feat = [(fl >= 0) & (fl < N_FEAT) for fl in feat_lane]
    own = [(lane < DA_DK) == (c == 0) for c in range(2)]

    @pl.when(qi == 0)
    def _():
        r_hi = ((row >> 4) << 4).astype(F32)
        r_lo = (row & 15).astype(F32)
        kn = [jnp.zeros((1, 1), F32), jnp.zeros((1, 1), F32)]
        for c in range(2):
            fk = jnp.where((feat_lane[c] & 1) == 0, r_hi, r_lo).astype(BF16)
            for j in range(n):
                kj = k_ref[j * t:(j + 1) * t, :]
                ka_ref[c, j * t:(j + 1) * t, :] = jnp.where(is_feat[c], fk, kj)
                k2 = jnp.where(own[c], kj.astype(F32), 0.0)
                kn[c] = jnp.maximum(kn[c], jnp.max(jnp.sum(k2 * k2, axis=1, keepdims=True), axis=0, keepdims=True))
        kn_ref[...] = jnp.concatenate([jnp.broadcast_to(kn[0], (1, LANES)), jnp.broadcast_to(kn[1], (1, LANES))], axis=0)
        rel = lax.broadcasted_iota(jnp.int32, (t, t), 1) - lax.broadcasted_iota(jnp.int32, (t, t), 0)
        e_ref[...] = jnp.abs(rel).astype(F32) * (-slope2)

    m_ref[...] = jnp.full_like(m_ref, -jnp.inf)
    acc_ref[...] = jnp.zeros_like(acc_ref)
    q = q_ref[...].astype(F32) * (DA_DK ** -0.5 * LOG2E)
    sv = jnp.full((t, LANES), slope2, F32)
    s_hi = sv.astype(BF16).astype(F32)
    s_mid = (sv - s_hi).astype(BF16).astype(F32)
    s_lo = (sv - s_hi - s_mid).astype(BF16).astype(F32)
    qn = []
    for c in range(2):
        fl = feat_lane[c]
        feat_q = jnp.where(is_feat[c], jnp.where(fl < 2, s_hi, jnp.where(fl < 4, s_mid, s_lo)), 0.0)
        qc = jnp.where(own[c], q, 0.0).astype(BF16)
        qa_ref[0 + c] = jnp.where(own[c], q, feat_q).astype(BF16)
        qa_ref[2 + c] = jnp.where(own[c], q, -feat_q).astype(BF16)
        qa_ref[4 + c] = qc
        q2 = qc.astype(F32)
        qn.append(jnp.max(jnp.sum(q2 * q2, axis=1, keepdims=True), axis=0, keepdims=True))

    near = jnp.where(qi > 0, qi - 1, 0)
    n_off = n - 1

    def tile_of(u, lo):
        o = jnp.where(u == 1, near, lo + (u - 2) + (lo + (u - 2) >= near).astype(jnp.int32))
        return jnp.where(u == 0, qi, o + (o >= qi).astype(jnp.int32))

    def scores_to(slot, u, lo):
        j = tile_of(u, lo)
        qsel = jnp.where(j == qi, 4, jnp.where(j < qi, 0, 2))
        r0 = pl.multiple_of(j * t, t)
        for c, hf in _ATT_CHAINS:
            qa = qa_ref[qsel + c, hf * half:(hf + 1) * half, :]
            slot[c, :, hf * half:(hf + 1) * half] = lax.dot_general(ka_ref[c, pl.ds(r0, t), :], qa, _NT,
                                                                      preferred_element_type=F32)

    def consume(slot, u, lo, diagonal=False):
        j = tile_of(u, lo)
        vt = vt_ref[:, pl.ds(pl.multiple_of(j * t, t), t)]
        if diagonal:
            colterm = jnp.zeros((1, t), F32)
        else:
            sign = jnp.where(j < qi, -slope2, slope2)
            col = lax.broadcasted_iota(jnp.int32, (1, t), 1) + (qi - j) * t
            colterm = col.astype(F32) * sign
        for c, hf in _ATT_CHAINS:
            cols = slice(hf * half, (hf + 1) * half)
            s = slot[c, :, cols]
            if diagonal:
                s = s + e_ref[:, cols]
            _attn_chain(s, colterm[:, cols], vt, m_ref, acc_ref, c, cols)

    i32 = jnp.int32
    zero = i32(0)
    scores_to(s0_ref, i32(0), zero)
    if n > 1:
        scores_to(s1_ref, i32(1), zero)
    consume(s0_ref, i32(0), zero, diagonal=True)
    if n > 1:
        reach = jnp.zeros((1, 1), F32)
        for c in range(2):
            m_min = jnp.min(m_ref[c], axis=1, keepdims=True)
            x = 0.5 * (qn[c] + kn_ref[c:c + 1, 0:1]) - m_min + UNDERFLOW
            reach = jnp.maximum(reach, x)
        d_f = jnp.floor(reach / (slope2 * t) - (1.0 / t)) + 1.0
        d_max = jnp.clip(d_f, 0.0, float(n)).astype(jnp.int32)[0, 0]
        lo = jnp.maximum(qi - d_max, 0)
        hi = jnp.minimum(qi + d_max - 1, n_off - 1)
        count = jnp.maximum(hi - lo + 1, 0)
        trips = jnp.maximum(count - 1, 0) // 2

        def pair(p, carry):
            u = 1 + 2 * p
            scores_to(s0_ref, u + 1, lo)
            consume(s1_ref, u, lo)
            scores_to(s1_ref, u + 2, lo)
            consume(s0_ref, u + 1, lo)
            return carry

        lax.fori_loop(0, trips, pair, 0)
        rem = count - 2 * trips

        @pl.when(rem == 1)
        def _():
            consume(s1_ref, count, lo)

        @pl.when(rem == 2)
        def _():
            scores_to(s0_ref, count, lo)
            consume(s1_ref, count - 1, lo)
            consume(s0_ref, count, lo)

    lam = sc_ref[DA_HEADS]
    a0, a1 = acc_ref[0], acc_ref[1]
    o = a0[:DA_DV] / a0[DA_DV:DA_DV + 1] - lam * (a1[:DA_DV] / a1[DA_DV:DA_DV + 1])
    o = o * lax.rsqrt(jnp.mean(o * o, axis=0, keepdims=True) + EPS) * (g_ref[...] * out_scale)
    o_ref[...] = o.T.astype(o_ref.dtype)


def _diff_attn(proj3, vt, scalars, g_da_col, t, out_scale):
    B, S, _ = proj3.shape
    n = S // t
    return pl.pallas_call(
        functools.partial(_attn_kernel, t=t, n=n, out_scale=out_scale),
        grid_spec=pltpu.PrefetchScalarGridSpec(
            num_scalar_prefetch=1,
            grid=(B, DA_HEADS, n),
            in_specs=[pl.BlockSpec((None, t, LANES), lambda b, h, i, sc: (b, i, CB_DQ + h)),
                      pl.BlockSpec((None, S, LANES), lambda b, h, i, sc: (b, 0, CB_DK + h)),
                      pl.BlockSpec((VT_ROWS, S), lambda b, h, i, sc: (h, b)),
                      pl.BlockSpec((DA_DV, 1), lambda b, h, i, sc: (0, 0))],
            out_specs=pl.BlockSpec((None, t, LANES), lambda b, h, i, sc: (b, i, h)),
            scratch_shapes=[pltpu.VMEM((6, t, LANES), BF16), pltpu.VMEM((2, S, LANES), BF16),
                            pltpu.VMEM((t, t), F32), pltpu.VMEM((2, LANES), F32),
                            pltpu.VMEM((2, t, t), F32), pltpu.VMEM((2, t, t), F32),
                            pltpu.VMEM((2, 1, t), F32), pltpu.VMEM((2, VT_ROWS, t), F32)]),
        out_shape=jax.ShapeDtypeStruct((B, S, DA_WIDTH), BF16),
        compiler_params=_cparams(("parallel", "parallel", "arbitrary")),
        name="diff_attn",
    )(scalars, proj3, proj3, vt, g_da_col)


def _memkv_kernel(m_ref, g_ref, w_ref, o_ref):
    h = _rms(m_ref[...], g_ref[...]).astype(BF16)
    o_ref[...] = jnp.dot(h, w_ref[...], preferred_element_type=F32).astype(BF16)


def _memkv(mem2d, g_mem, w_ckv_bf):
    T = mem2d.shape[0]
    return pl.pallas_call(
        _memkv_kernel,
        grid=(T // N_MEM,),
        in_specs=[pl.BlockSpec((N_MEM, D_MODEL), lambda i: (i, 0)),
                  pl.BlockSpec((1, D_MODEL), lambda i: (0, 0)),
                  pl.BlockSpec((D_MODEL, 2 * D_MODEL), lambda i: (0, 0))],
        out_specs=pl.BlockSpec((N_MEM, 2 * D_MODEL), lambda i: (i, 0)),
        out_shape=jax.ShapeDtypeStruct((T, 2 * D_MODEL), BF16),
        compiler_params=_cparams(("parallel",)),
        name="mem_kv",
    )(mem2d, g_mem, w_ckv_bf)


def _post_kernel(x_ref, of_ref, ob_ref, hgate_ref, oda_ref, kv_ref, ghg_ref, wout_ref, gca_ref, wcq_ref, wco_ref,
                 gffn_ref, wr_ref, br_ref,
                 x2_ref, hn_ref, ri_ref, rw_ref, cnt_ref, carry_ref, *, tm):
    @pl.when(pl.program_id(0) == 0)
    def _():
        carry_ref[...] = jnp.zeros_like(carry_ref)

    hm = tm // 2
    halves = [slice(0, hm), slice(hm, tm)]
    f32dot = functools.partial(jnp.dot, preferred_element_type=F32)

    ghg = ghg_ref[...]
    mix_hg = []
    for r in halves:
        o_hg = of_ref[r, :].astype(F32) + ob_ref[r, :].astype(F32)
        parts = [_rms(o_hg[:, i * HG_D:(i + 1) * HG_D], ghg[:, i * HG_D:(i + 1) * HG_D]) for i in range(HG_HEADS)]
        mix_hg.append((jnp.concatenate(parts, axis=-1) * _silu(hgate_ref[r, :].astype(F32))).astype(BF16))
    x1 = [x_ref[r, :] + f32dot(mh, wout_ref[:HG_WIDTH, :]) + f32dot(oda_ref[r, :], wout_ref[HG_WIDTH:, :])
          for r, mh in zip(halves, mix_hg)]

    hq_in = [_rms(v, gca_ref[...]).astype(BF16) for v in x1]
    hq = [f32dot(v, wcq_ref[...]).astype(BF16) for v in hq_in]
    scores = [[lax.dot_general(v[:, i * CA_DH:(i + 1) * CA_DH], kv_ref[:, i * CA_DH:(i + 1) * CA_DH], _NT,
                               preferred_element_type=F32) * (CA_DH ** -0.5) for i in range(CA_HEADS)] for v in hq]
    probs = []
    for sc in scores:
        ps = []
        for s_ in sc:
            e = jnp.exp(s_ - jnp.max(s_, axis=-1, keepdims=True))
            ps.append((e / jnp.sum(e, axis=-1, keepdims=True)).astype(BF16))
        probs.append(ps)
    ca = [jnp.concatenate([f32dot(ps[i], kv_ref[:, D_MODEL + i * CA_DH:D_MODEL + (i + 1) * CA_DH]).astype(BF16)
                           for i in range(CA_HEADS)], axis=-1) for ps in probs]
    x2 = [v + f32dot(c_, wco_ref[...]) for v, c_ in zip(x1, ca)]

    hn = [_rms(v, gffn_ref[...]) for v in x2]
    for r, v2, vn in zip(halves, x2, hn):
        x2_ref[r, :] = v2
        hn_ref[r, :] = _pack_bf16_pairs(vn)
    logits = []
    for vn in hn:
        hi = vn.astype(BF16)
        lo = (vn - hi.astype(F32)).astype(BF16)
        logits.append(f32dot(hi, wr_ref[0]) + f32dot(lo, wr_ref[0]) + f32dot(hi, wr_ref[1]) + br_ref[...])
    lane = lax.broadcasted_iota(jnp.int32, (hm, LANES), 1)
    big = jnp.int32(1 << 20)
    neg = jnp.float32(-jnp.inf)

    def first_max(vals):
        m = jnp.max(vals, axis=-1, keepdims=True)
        return m, jnp.min(jnp.where(vals == m, lane, big), axis=-1, keepdims=True)

    routed = []
    for lg in logits:
        gl = jnp.where(lane < N_GROUPS, lg, neg)
        gmax, gidx = first_max(gl)
        g_val = 1.0 / jnp.sum(jnp.exp(gl - gmax), axis=-1, keepdims=True)
        e_lane = lane - N_GROUPS
        in_grp = (e_lane >= 0) & (e_lane < N_EXPERTS) & ((e_lane // EXP_PER_GROUP) == gidx)
        el = jnp.where(in_grp, lg, neg)
        m1, i1 = first_max(el)
        m2, i2 = first_max(jnp.where(lane == i1, neg, el))
        t = jnp.exp(m2 - m1)
        routed.append((i1, i2, g_val / (1.0 + t), g_val * t / (1.0 + t)))

    r_i = lax.broadcasted_iota(jnp.int32, (hm, hm), 0)
    c_i = lax.broadcasted_iota(jnp.int32, (hm, hm), 1)
    prefix = (c_i < r_i).astype(BF16)
    carry = carry_ref[...]
    for r, (i1, i2, w1, w2) in zip(halves, routed):
        hit1, hit2 = lane == i1, lane == i2
        onehot = (hit1 | hit2).astype(BF16)
        before = f32dot(prefix, onehot) + carry
        r1 = jnp.sum(jnp.where(hit1, before, 0.0), axis=-1, keepdims=True).astype(jnp.int32)
        r2 = jnp.sum(jnp.where(hit2, before, 0.0), axis=-1, keepdims=True).astype(jnp.int32)
        carry = carry + jnp.sum(onehot.astype(F32), axis=0, keepdims=True)
        ri_ref[r, :] = jnp.where(lane == 0, i1 - N_GROUPS, jnp.where(lane == 1, i2 - N_GROUPS,
                                 jnp.where(lane == 2, r1, jnp.where(lane == 3, r2, 0))))
        rw_ref[r, :] = jnp.where(lane == 0, w1, jnp.where(lane == 1, w2, 0.0))
    carry_ref[...] = carry
    cnt_ref[...] = carry


def _post(x2d, o_f, o_b, proj, o_da, kv3, S, w, tm):
    T = x2d.shape[0]
    per_b = S // tm
    row = lambda i: (i, 0)
    const = lambda i: (0, 0)
    return pl.pallas_call(
        functools.partial(_post_kernel, tm=tm),
        grid=(T // tm,),
        in_specs=[pl.BlockSpec((tm, D_MODEL), row),
                  pl.BlockSpec((tm, HG_WIDTH), row),
                  pl.BlockSpec((tm, HG_WIDTH), row),
                  pl.BlockSpec((tm, HG_WIDTH), lambda i: (i, CW_HGATE)),
                  pl.BlockSpec((tm, DA_WIDTH), row),
                  pl.BlockSpec((None, N_MEM, 2 * D_MODEL), lambda i: (i // per_b, 0, 0)),
                  pl.BlockSpec((1, HG_WIDTH), const),
                  pl.BlockSpec((D_MODEL, D_MODEL), const),
                  pl.BlockSpec((1, D_MODEL), const),
                  pl.BlockSpec((D_MODEL, D_MODEL), const),
                  pl.BlockSpec((D_MODEL, D_MODEL), const),
                  pl.BlockSpec((1, D_MODEL), const),
                  pl.BlockSpec((2, D_MODEL, LANES), lambda i: (0, 0, 0)),
                  pl.BlockSpec((1, LANES), const)],
        out_specs=[pl.BlockSpec((tm, D_MODEL), row),
                   pl.BlockSpec((tm, D_PACK), row),
                   pl.BlockSpec((tm, LANES), row),
                   pl.BlockSpec((tm, LANES), row),
                   pl.BlockSpec((1, LANES), const)],
        out_shape=[jax.ShapeDtypeStruct((T, D_MODEL), F32),
                   jax.ShapeDtypeStruct((T, D_PACK), jnp.uint32),
                   jax.ShapeDtypeStruct((T, LANES), jnp.int32),
                   jax.ShapeDtypeStruct((T, LANES), F32),
                   jax.ShapeDtypeStruct((1, LANES), F32)],
        scratch_shapes=[pltpu.VMEM((1, LANES), F32)],
        compiler_params=_cparams(("arbitrary",)),
        name="post_mixer",
    )(x2d, o_f, o_b, proj, o_da, kv3, w["g_hg"], w["w_out"], w["g_ca"], w["w_cq"], w["w_co"], w["g_ffn"],
      w["w_r"], w["b_r"])


GATHER_SLOTS = 3


def _expert_kernel(be_ref, nu_ref, src_ref, src_prev_ref, tok_ref, tok_next_ref, tok_next2_ref, hn_ref, wg_ref, wu_ref,
                   wd_ref, yt_ref, xbuf_ref, ybuf_ref, zbuf_ref, gsem, ssem, *, n_tok):
    del be_ref
    i = pl.program_id(0)
    n_used = nu_ref[0]
    gslot = i % GATHER_SLOTS
    yslot = i % 2

    def gather(tok, buf_slot, r):
        return pltpu.make_async_copy(hn_ref.at[pl.ds(tok[0, r], 1), :], xbuf_ref.at[buf_slot, pl.ds(r, 1), :],
                                     gsem.at[buf_slot])

    def scatter(src, buf_slot, r):
        return pltpu.make_async_copy(ybuf_ref.at[buf_slot, pl.ds(r, 1), :], yt_ref.at[pl.ds(src[0, r], 1), :],
                                     ssem.at[buf_slot])

    def zero_fill(k):
        return pltpu.make_async_copy(zbuf_ref, yt_ref.at[pl.ds(2 * n_tok + k * MOE_BLOCK, MOE_BLOCK), :], ssem.at[0])

    def for_rows(fn):
        def body(r, c):
            fn(r)
            return c
        lax.fori_loop(0, MOE_BLOCK, body, 0, unroll=8)

    @pl.when((i == 0) & (n_used > 0))
    def _():
        ybuf_ref[...] = jnp.zeros_like(ybuf_ref)
        zbuf_ref[...] = jnp.zeros_like(zbuf_ref)
        n_fill = (yt_ref.shape[0] - 2 * n_tok) // MOE_BLOCK
        for k in range(n_fill):
            zero_fill(k).start()
        for k in range(n_fill):
            zero_fill(k).wait()
        for_rows(lambda r: gather(tok_ref, 0, r).start())
        for_rows(lambda r: gather(tok_next_ref, 1, r).start())

    @pl.when(i < n_used)
    def _():
        for_rows(lambda r: gather(tok_ref, gslot, r).wait())
        for r in range(MOE_BLOCK):
            gather(tok_next2_ref, (i + 2) % GATHER_SLOTS, r).start()
            scatter(src_prev_ref, 1 - yslot, r).start()
        xb = _unpack_bf16_pairs(xbuf_ref[gslot]).astype(BF16)
        a = jnp.dot(xb, wg_ref[...], preferred_element_type=F32)
        u = jnp.dot(xb, wu_ref[...], preferred_element_type=F32)
        hmid = (_silu(a) * u).astype(BF16)
        y = _pack_bf16_pairs(jnp.dot(hmid, wd_ref[...], preferred_element_type=F32))

        @pl.when(i > 0)
        def _():
            for_rows(lambda r: scatter(src_ref, yslot, r).wait())

        ybuf_ref[yslot] = y

        @pl.when(i == n_used - 1)
        def _():
            for_rows(lambda r: scatter(src_ref, yslot, r).start())
            for_rows(lambda r: scatter(src_ref, yslot, r).wait())
            for_rows(lambda r: scatter(src_ref, 1 - yslot, r).wait())
            for_rows(lambda r: gather(tok_ref, (i + 1) % GATHER_SLOTS, r).wait())
            for_rows(lambda r: gather(tok_ref, (i + 2) % GATHER_SLOTS, r).wait())


def _experts(hn, slot_src, block_expert, n_used, w_gate, w_up, w_down):
    n_tok = hn.shape[0]
    n_slots = slot_src.shape[0]
    nb = n_slots // MOE_BLOCK
    lead = 2 * n_tok + jnp.arange(MOE_BLOCK, dtype=jnp.int32)
    src3 = jnp.concatenate([lead, slot_src]).reshape(nb + 1, 1, MOE_BLOCK)
    tok3 = jnp.where(src3 < n_tok, src3, jnp.where(src3 < 2 * n_tok, src3 - n_tok, n_tok - 1))
    n_rows = n_slots + MOE_BLOCK

    def src_spec(off):
        return pl.BlockSpec((None, 1, MOE_BLOCK), lambda i, be, nu: (jnp.minimum(i + 1 + off, nb), 0, 0),
                            memory_space=pltpu.SMEM)

    return pl.pallas_call(
        functools.partial(_expert_kernel, n_tok=n_tok),
        grid_spec=pltpu.PrefetchScalarGridSpec(
            num_scalar_prefetch=2,
            grid=(nb,),
            in_specs=[src_spec(0), src_spec(-1), src_spec(0), src_spec(1), src_spec(2),
                      pl.BlockSpec(memory_space=pl.ANY),
                      pl.BlockSpec((None, D_MODEL, D_EXPERT), lambda i, be, nu: (be[i], 0, 0)),
                      pl.BlockSpec((None, D_MODEL, D_EXPERT), lambda i, be, nu: (be[i], 0, 0)),
                      pl.BlockSpec((None, D_EXPERT, D_MODEL), lambda i, be, nu: (be[i], 0, 0))],
            out_specs=pl.BlockSpec(memory_space=pl.ANY),
            scratch_shapes=[pltpu.VMEM((GATHER_SLOTS, MOE_BLOCK, D_PACK), jnp.uint32),
                            pltpu.VMEM((2, MOE_BLOCK, D_PACK), jnp.uint32),
                            pltpu.VMEM((MOE_BLOCK, D_PACK), jnp.uint32),
                            pltpu.SemaphoreType.DMA((GATHER_SLOTS,)), pltpu.SemaphoreType.DMA((2,))]),
        out_shape=jax.ShapeDtypeStruct((n_rows, D_PACK), jnp.uint32),
        compiler_params=_cparams(("arbitrary",), has_side_effects=True),
        name="moe_experts",
    )(block_expert, n_used, src3, src3, tok3, tok3, tok3, hn, w_gate, w_up, w_down)


def _combine_kernel(x2_ref, rw_ref, g_ref, y1_ref, y2_ref, y_ref):
    rw = rw_ref[...]
    lane = lax.broadcasted_iota(jnp.int32, rw.shape, 1)
    w1 = jnp.sum(jnp.where(lane == 0, rw, 0.0), axis=-1, keepdims=True)
    w2 = jnp.sum(jnp.where(lane == 1, rw, 0.0), axis=-1, keepdims=True)
    x3 = x2_ref[...] + (_unpack_bf16_pairs(y1_ref[...]) * w1 + _unpack_bf16_pairs(y2_ref[...]) * w2)
    y_ref[...] = _rms(x3, g_ref[...])


def _combine(x2, rw, g_final, yt, tm):
    T = x2.shape[0]
    return pl.pallas_call(
        _combine_kernel,
        grid=(T // tm,),
        in_specs=[pl.BlockSpec((tm, D_MODEL), lambda i: (i, 0)),
                  pl.BlockSpec((tm, LANES), lambda i: (i, 0)),
                  pl.BlockSpec((1, D_MODEL), lambda i: (0, 0)),
                  pl.BlockSpec((tm, D_PACK), lambda i: (i, 0)),
                  pl.BlockSpec((tm, D_PACK), lambda i: (i + T // tm, 0))],
        out_specs=pl.BlockSpec((tm, D_MODEL), lambda i: (i, 0)),
        out_shape=jax.ShapeDtypeStruct((T, D_MODEL), F32),
        compiler_params=_cparams(("parallel",)),
        name="moe_combine",
    )(x2, rw, g_final, yt, yt)


def _tile(n, pref):
    return min(n, pref)


def _trunk(x, mem, w):
    B, S, _ = x.shape
    T = B * S
    x2d = x.reshape(T, D_MODEL)
    proj, vt = _inproj(x2d, w["g_mix"], w["w_in"], w["w_vt"], _tile(T, PROJ_TM))
    proj3 = proj.reshape(B, S, IN_COLS)
    o_f, o_b = _hgrn(proj3, w["lb"], _tile(S, HG_ROWS))
    o_da = _diff_attn(proj3, vt, w["attn_scalars"], w["g_da"], _tile(S, ATT_T), w["da_out_scale"])
    kv = _memkv(mem.reshape(B * N_MEM, D_MODEL), w["g_mem"], w["w_ckv"]).reshape(B, N_MEM, 2 * D_MODEL)
    tm = _tile(S, POST_TM)
    x2, hn, ri, rw, cnt = _post(x2d, o_f.reshape(T, HG_WIDTH), o_b.reshape(T, HG_WIDTH), proj,
                                o_da.reshape(T, DA_WIDTH), kv, S, w, tm)

    counts = cnt[0, N_GROUPS:N_GROUPS + N_EXPERTS].astype(jnp.int32)
    padded = (counts + MOE_BLOCK - 1) // MOE_BLOCK * MOE_BLOCK
    pad_end = jnp.cumsum(padded)
    pad_start = pad_end - padded
    dest = pad_start[ri[:, 0:2]] + ri[:, 2:4]
    n_slots = 2 * T + N_EXPERTS * MOE_BLOCK
    nb = n_slots // MOE_BLOCK
    block_start = jnp.arange(nb, dtype=jnp.int32) * MOE_BLOCK
    block_expert = jnp.minimum(jnp.sum((pad_end[None, :] <= block_start[:, None]).astype(jnp.int32), axis=1),
                               N_EXPERTS - 1)
    n_used = (pad_end[-1:] // MOE_BLOCK).astype(jnp.int32)
    e_slot = jnp.repeat(block_expert, MOE_BLOCK)
    pad_before = jnp.cumsum(padded - counts) - (padded - counts)
    pad_rank = pad_before[e_slot] + jnp.arange(n_slots, dtype=jnp.int32) - pad_start[e_slot] - counts[e_slot]
    assign = jnp.arange(2 * T, dtype=jnp.int32)
    slot_src = (2 * T + MOE_BLOCK + pad_rank).at[dest.reshape(-1)].set((assign & 1) * T + (assign >> 1))
    yt = _experts(hn, slot_src, block_expert, n_used, w["w_gate"], w["w_up"], w["w_down"])
    y = _combine(x2, rw, w["g_final"], yt, tm)
    return y.reshape(B, S, D_MODEL)


def _split_bf16(a):
    hi = a.astype(BF16)
    return jnp.stack([hi, (a - hi.astype(F32)).astype(BF16)])


def kernel(x_prompt, x_sample, mem_prompt, mem_sample, g_mix, w_in, hg_lb, g_hg, lam_q1, lam_k1, lam_q2, lam_k2,
           g_da, w_out, g_ca, g_mem, w_cq, w_ckv, w_co, g_ffn, w_rg, b_rg, w_re, b_re, w_gate, w_up, w_down, g_final):
    l = 0
    lam_init = 0.8 - 0.6 * math.exp(-0.3 * l)
    lam = (jnp.exp(jnp.sum(lam_q1[l] * lam_k1[l])) - jnp.exp(jnp.sum(lam_q2[l] * lam_k2[l])) + lam_init)
    slopes = jnp.power(2.0, -8.0 * jnp.arange(1, DA_HEADS + 1, dtype=F32) / DA_HEADS)
    pad = LANES - N_GROUPS - N_EXPERTS
    w = {
        "g_mix": g_mix[l][None], "w_in": w_in[l].astype(BF16),
        "w_vt": w_in[l][:, CB_DV * LANES:].T.astype(BF16),
        "lb": jnp.cumsum(jax.nn.softmax(hg_lb, axis=0), axis=0)[l],
        "g_hg": g_hg[l][None],
        "attn_scalars": jnp.concatenate([slopes, lam[None]]).astype(F32),
        "g_da": g_da[l][:, None], "da_out_scale": 1.0 - lam_init,
        "w_out": w_out[l].astype(BF16), "g_ca": g_ca[l][None], "g_mem": g_mem[l][None],
        "w_cq": w_cq[l].astype(BF16), "w_ckv": w_ckv[l].astype(BF16), "w_co": w_co[l].astype(BF16),
        "g_ffn": g_ffn[l][None],
        "w_r": _split_bf16(jnp.pad(jnp.concatenate([w_rg[l], w_re[l]], axis=1), ((0, 0), (0, pad)))),
        "b_r": jnp.pad(jnp.concatenate([b_rg[l], b_re[l]]), (0, pad))[None],
        "w_gate": w_gate[l].astype(BF16), "w_up": w_up[l].astype(BF16), "w_down": w_down[l].astype(BF16),
        "g_final": g_final[None],
    }
    return (_trunk(x_prompt, mem_prompt, w), _trunk(x_sample, mem_sample, w))
```

```python
import functools
import math

import jax
import jax.numpy as jnp
from jax import lax
from jax.experimental import pallas as pl
from jax.experimental.pallas import tpu as pltpu

F32 = jnp.float32
BF16 = jnp.bfloat16

D_MODEL = 1024
N_MEM = 256
EPS = 1e-6
HG_HEADS = 4
HG_D = 128
HG_WIDTH = HG_HEADS * HG_D
DA_HEADS = 4
DA_DK = 64
DA_DV = 128
DA_WIDTH = DA_HEADS * DA_DV
IN_COLS = 4096
CA_HEADS = 4
CA_DH = 256
N_GROUPS = 4
EXP_PER_GROUP = 8
N_EXPERTS = 32
D_EXPERT = 512
LANES = 128
D_PACK = D_MODEL // 2
VT_ROWS = DA_DV + 16

CW_HQ, CW_FF, CW_FB, CW_HI, CW_HGATE = 0, 1, 2, 3, 4
CB_DQ, CB_DK, CB_DV = 20, 24, 28

HG_CHUNK = 64
HG_SUB = 16
EXP_CLAMP = 80.0
MOE_BLOCK = 256

VMEM_LIMIT = 48 * 1024 * 1024

PROJ_TM = 512
HG_ROWS = 256
ATT_T = 512
POST_TM = 512


def _cparams(sem, **kw):
    return pltpu.CompilerParams(dimension_semantics=sem, vmem_limit_bytes=VMEM_LIMIT, **kw)


def _rms(x, g):
    return x * lax.rsqrt(jnp.mean(x * x, axis=-1, keepdims=True) + EPS) * g


def _silu(x):
    return x * (1.0 / (1.0 + jnp.exp(-x)))


def _pack_bf16_pairs(x):
    n = x.shape[1] // 2
    lo = lax.bitcast_convert_type(x[:, :n].astype(BF16).astype(F32), jnp.uint32)
    hi = lax.bitcast_convert_type(x[:, n:].astype(BF16).astype(F32), jnp.uint32)
    return (lo >> 16) | (hi & jnp.uint32(0xFFFF0000))


def _unpack_bf16_pairs(w):
    lo = lax.bitcast_convert_type(w << 16, F32)
    hi = lax.bitcast_convert_type(w & jnp.uint32(0xFFFF0000), F32)
    return jnp.concatenate([lo, hi], axis=1)


def _inproj_kernel(x_ref, g_ref, w_ref, wvt_ref, o_ref, vt_ref, *, n_chunk):
    h = _rms(x_ref[...], g_ref[...]).astype(BF16)
    for j in range(IN_COLS // n_chunk):
        sl = slice(j * n_chunk, (j + 1) * n_chunk)
        o_ref[:, sl] = jnp.dot(h, w_ref[:, sl], preferred_element_type=F32).astype(BF16)
    vt = lax.dot_general(wvt_ref[...], h, (((1,), (1,)), ((), ())), preferred_element_type=F32).astype(BF16)
    for hd in range(DA_HEADS):
        vt_ref[hd * VT_ROWS:hd * VT_ROWS + DA_DV, :] = vt[hd * DA_DV:(hd + 1) * DA_DV]
        vt_ref[hd * VT_ROWS + DA_DV:(hd + 1) * VT_ROWS, :] = jnp.ones((VT_ROWS - DA_DV, vt.shape[1]), BF16)


def _inproj(x2d, g_mix, w_in_bf, w_vt_bf, tm):
    T = x2d.shape[0]
    return pl.pallas_call(
        functools.partial(_inproj_kernel, n_chunk=512),
        grid=(T // tm,),
        in_specs=[pl.BlockSpec((tm, D_MODEL), lambda i: (i, 0)),
                  pl.BlockSpec((1, D_MODEL), lambda i: (0, 0)),
                  pl.BlockSpec((D_MODEL, IN_COLS), lambda i: (0, 0)),
                  pl.BlockSpec((DA_WIDTH, D_MODEL), lambda i: (0, 0))],
        out_specs=[pl.BlockSpec((tm, IN_COLS), lambda i: (i, 0)),
                   pl.BlockSpec((DA_HEADS * VT_ROWS, tm), lambda i: (0, i))],
        out_shape=[jax.ShapeDtypeStruct((T, IN_COLS), BF16), jax.ShapeDtypeStruct((DA_HEADS * VT_ROWS, T), BF16)],
        compiler_params=_cparams(("parallel",)),
        name="inproj",
    )(x2d, g_mix, w_in_bf, w_vt_bf)


def _sigmoid(x):
    return 0.5 + 0.5 * jnp.tanh(0.5 * x)


def _hgrn_prep(dirs, lb, tri2):
    C, S = HG_CHUNK, HG_SUB
    n_sub = C // S
    stage1 = []
    for d, (q_raw, f_raw, v) in enumerate(dirs):
        xq = q_raw.astype(F32)
        q = xq * _sigmoid(xq)
        f = lb + (1.0 - lb) * _sigmoid(f_raw.astype(F32))
        lf = jnp.log(f)
        hi = lf.astype(BF16)
        mid = (lf - hi.astype(F32)).astype(BF16)
        cc = jnp.dot(tri2[d], hi, preferred_element_type=F32) + jnp.dot(tri2[d], mid, preferred_element_type=F32)
        stage1.append((q, 1.0 - f, v, cc[:C], cc[C:]))
    stage2 = []
    for d, (q, k, v, cum, cum_loc) in enumerate(stage1):
        ref_pt = cum - cum_loc
        edge = cum[0:1, :] if d == 1 else cum[C - 1:C, :]
        q_inter = (q * jnp.exp(cum)).astype(BF16)
        q_loc = (q * jnp.exp(cum_loc)).astype(BF16)
        k_st = (k * jnp.exp(edge - cum)).astype(BF16)
        k_sub = [(k * jnp.exp(jnp.minimum(ref_pt[i * S:i * S + 1, :] - cum, EXP_CLAMP))).astype(BF16)
                 for i in range(n_sub)]
        peak = jnp.max(jnp.max(-cum_loc, axis=1, keepdims=True), axis=0, keepdims=True)
        stage2.append((q_inter, q_loc, k_st, k_sub, jnp.exp(edge), v, q, k, cum, peak))
    return stage2


def _hgrn_exact_scores(q, k, cum, keep_d, xq_ref, xk_ref, xc_ref):
    C = HG_CHUNK
    xq_ref[...] = q
    xk_ref[...] = k
    xc_ref[...] = cum
    lane = lax.broadcasted_iota(jnp.int32, (C, C), 1)

    def body(s_, accs):
        w = (jnp.exp(jnp.minimum(xc_ref[...] - xc_ref[pl.ds(s_, 1), :], 0.0)) * xq_ref[...]) * xk_ref[pl.ds(s_, 1), :]
        return tuple(jnp.where(lane == s_, jnp.sum(w[:, h * HG_D:(h + 1) * HG_D], axis=1, keepdims=True), accs[h])
                     for h in range(HG_HEADS))

    accs = lax.fori_loop(0, C, body, tuple(jnp.zeros((C, C), F32) for _ in range(HG_HEADS)))
    return [jnp.where(keep_d, a, 0.0).astype(BF16) for a in accs]


def _hgrn_apply(stage2, st_refs, keep, exact_refs=None):
    C, S = HG_CHUNK, HG_SUB
    n_sub = C // S
    nt = (((1,), (1,)), ((), ()))
    chains = [(d, h) for d in range(2) for h in range(HG_HEADS)]
    hs = lambda h: slice(h * HG_D, (h + 1) * HG_D)
    st_old, o_inter, scores = {}, {}, {}
    if exact_refs is not None:
        for d in range(2):
            ex = _hgrn_exact_scores(stage2[d][6], stage2[d][7], stage2[d][8], keep[d], *exact_refs)
            for h in range(HG_HEADS):
                scores[d, h] = ex[h]
    for d, h in chains:
        q_inter, q_loc, _, k_sub = stage2[d][:4]
        st_old[d, h] = st_refs[d][h]
        o_inter[d, h] = lax.dot_general(q_inter[:, hs(h)], st_old[d, h].astype(BF16), nt, preferred_element_type=F32)
        if exact_refs is None:
            sc = jnp.concatenate([lax.dot_general(q_loc[i * S:(i + 1) * S, hs(h)], k_sub[i][:, hs(h)], nt,
                                                  preferred_element_type=F32) for i in range(n_sub)], axis=0)
            scores[d, h] = jnp.where(keep[d], sc, 0.0).astype(BF16)
    outs = {}
    for d, h in chains:
        outs[d, h] = o_inter[d, h] + jnp.dot(scores[d, h], stage2[d][5][:, hs(h)], preferred_element_type=F32)
    for d, h in chains:
        k_st, dec, v = stage2[d][2], stage2[d][4], stage2[d][5]
        st_refs[d][h] = st_old[d, h] * dec[:, hs(h)] + lax.dot_general(
            v[:, hs(h)], k_st[:, hs(h)], (((0,), (0,)), ((), ())), preferred_element_type=F32)
    return [jnp.concatenate([outs[d, h] for h in range(HG_HEADS)], axis=-1) for d in range(2)]


def _hgrn_kernel(lb_ref, qf_ref, ff_ref, vf_ref, qb_ref, fb_ref, vb_ref, of_ref, ob_ref, stf_ref, stb_ref,
                 xq_ref, xk_ref, xc_ref, *, rows):
    @pl.when(pl.program_id(1) == 0)
    def _():
        stf_ref[...] = jnp.zeros_like(stf_ref)
        stb_ref[...] = jnp.zeros_like(stb_ref)

    C, S = HG_CHUNK, HG_SUB
    r = lax.broadcasted_iota(jnp.int32, (2 * C, C), 0)
    c = lax.broadcasted_iota(jnp.int32, (2 * C, C), 1)
    rr = jnp.where(r >= C, r - C, r)
    local = (r < C) | ((rr // S) == (c // S))
    tri2 = [jnp.where((c <= rr) & local, 1.0, 0.0).astype(BF16), jnp.where((c >= rr) & local, 1.0, 0.0).astype(BF16)]
    r1 = lax.broadcasted_iota(jnp.int32, (C, C), 0)
    c1 = lax.broadcasted_iota(jnp.int32, (C, C), 1)
    keep = [c1 <= r1, c1 >= r1]
    lb = lb_ref[...]
    n = rows // C

    def body(jj, carry):
        offs = []
        for j in (2 * jj, 2 * jj + 1):
            offs.append((pl.multiple_of(j * C, C), pl.multiple_of((n - 1 - j) * C, C)))
        preps = [_hgrn_prep([(qf_ref[pl.ds(a, C), :], ff_ref[pl.ds(a, C), :], vf_ref[pl.ds(a, C), :]),
                             (qb_ref[pl.ds(b, C), :], fb_ref[pl.ds(b, C), :], vb_ref[pl.ds(b, C), :])], lb, tri2)
                 for a, b in offs]
        peak = functools.reduce(jnp.maximum, [st2[d][9] for st2 in preps for d in range(2)])
        clamped = peak[0, 0] > EXP_CLAMP

        def apply_all(exact_refs):
            for (a, b), st2 in zip(offs, preps):
                o_f, o_b = _hgrn_apply(st2, (stf_ref, stb_ref), keep, exact_refs)
                of_ref[pl.ds(a, C), :] = o_f.astype(of_ref.dtype)
                ob_ref[pl.ds(b, C), :] = o_b.astype(ob_ref.dtype)

        pl.when(jnp.logical_not(clamped))(lambda: apply_all(None))
        pl.when(clamped)(lambda: apply_all((xq_ref, xk_ref, xc_ref)))
        return carry

    lax.fori_loop(0, n // 2, body, 0)


def _hgrn(proj3, lb, rows):
    B, S, _ = proj3.shape
    n = S // rows
    W = HG_WIDTH

    def fwd(cb):
        return pl.BlockSpec((None, rows, W), lambda b, c: (b, c, cb))

    def bwd(cb):
        return pl.BlockSpec((None, rows, W), lambda b, c: (b, n - 1 - c, cb))

    out = jax.ShapeDtypeStruct((B, S, W), BF16)
    return pl.pallas_call(
        functools.partial(_hgrn_kernel, rows=rows),
        grid=(B, n),
        in_specs=[pl.BlockSpec((1, W), lambda b, c: (0, 0)),
                  fwd(CW_HQ), fwd(CW_FF), fwd(CW_HI), bwd(CW_HQ), bwd(CW_FB), bwd(CW_HI)],
        out_specs=[pl.BlockSpec((None, rows, W), lambda b, c: (b, c, 0)),
                   pl.BlockSpec((None, rows, W), lambda b, c: (b, n - 1 - c, 0))],
        out_shape=[out, out],
        scratch_shapes=[pltpu.VMEM((HG_HEADS, HG_D, HG_D), F32), pltpu.VMEM((HG_HEADS, HG_D, HG_D), F32),
                        pltpu.VMEM((HG_CHUNK, W), F32), pltpu.VMEM((HG_CHUNK, W), F32), pltpu.VMEM((HG_CHUNK, W), F32)],
        compiler_params=_cparams(("parallel", "arbitrary")),
        name="hgrn2",
    )(lb.reshape(1, W), proj3, proj3, proj3, proj3, proj3, proj3)


LOG2E = 1.4426950408889634
N_FEAT = 6
UNDERFLOW = 160.0


def _attn_chain(s, colterm, vt, m_ref, acc_ref, c, cols):
    m_old = m_ref[c, :, cols]
    m_new = jnp.maximum(m_old, jnp.max(s, axis=0, keepdims=True) + colterm)
    alpha = jnp.exp2(m_old - m_new)
    p = jnp.exp2(s - (m_new - colterm)).astype(BF16)
    acc_ref[c, :, cols] = alpha * acc_ref[c, :, cols] + jnp.dot(vt, p, preferred_element_type=F32)
    m_ref[c, :, cols] = m_new


_ATT_CHAINS = [(c, hf) for c in range(2) for hf in range(2)]
_NT = (((1,), (1,)), ((), ()))


def _attn_kernel(sc_ref, q_ref, k_ref, vt_ref, g_ref, o_ref, qa_ref, ka_ref, e_ref, kn_ref, s0_ref, s1_ref,
                 m_ref, acc_ref, *, t, n, out_scale):
    h, qi = pl.program_id(1), pl.program_id(2)
    slope2 = sc_ref[h] * LOG2E
    half = t // 2
    lane = lax.broadcasted_iota(jnp.int32, (t, LANES), 1)
    row = lax.broadcasted_iota(jnp.int32, (t, LANES), 0)
    feat_lane = [lane - (DA_DK if c == 0 else 0) for c in range(2)]
    is_feat = [(fl >= 0) & (fl < N_FEAT) for fl in feat_lane]
    own = [(lane < DA_DK) == (c == 0) for c in range(2)]

    @pl.when(qi == 0)
    def _():
        r_hi = ((row >> 4) << 4).astype(F32)
        r_lo = (row & 15).astype(F32)
        kn = [jnp.zeros((1, 1), F32), jnp.zeros((1, 1), F32)]
        for c in range(2):
            fk = jnp.where((feat_lane[c] & 1) == 0, r_hi, r_lo).astype(BF16)
            for j in range(n):
                kj = k_ref[j * t:(j + 1) * t, :]
                ka_ref[c, j * t:(j + 1) * t, :] = jnp.where(is_feat[c], fk, kj)
                k2 = jnp.where(own[c], kj.astype(F32), 0.0)
                kn[c] = jnp.maximum(kn[c], jnp.max(jnp.sum(k2 * k2, axis=1, keepdims=True), axis=0, keepdims=True))
        kn_ref[...] = jnp.concatenate([jnp.broadcast_to(kn[0], (1, LANES)), jnp.broadcast_to(kn[1], (1, LANES))], axis=0)
        rel = lax.broadcasted_iota(jnp.int32, (t, t), 1) - lax.broadcasted_iota(jnp.int32, (t, t), 0)
        e_ref[...] = jnp.abs(rel).astype(F32) * (-slope2)

    m_ref[...] = jnp.full_like(m_ref, -jnp.inf)
    acc_ref[...] = jnp.zeros_like(acc_ref)
    q = q_ref[...].astype(F32) * (DA_DK ** -0.5 * LOG2E)
    sv = jnp.full((t, LANES), slope2, F32)
    s_hi = sv.astype(BF16).astype(F32)
    s_mid = (sv - s_hi).astype(BF16).astype(F32)
    s_lo = (sv - s_hi - s_mid).astype(BF16).astype(F32)
    qn = []
    for c in range(2):
        fl = feat_lane[c]
        feat_q = jnp.where(is_feat[c], jnp.where(fl < 2, s_hi, jnp.where(fl < 4, s_mid, s_lo)), 0.0)
        qc = jnp.where(own[c], q, 0.0).astype(BF16)
        qa_ref[0 + c] = jnp.where(own[c], q, feat_q).astype(BF16)
        qa_ref[2 + c] = jnp.where(own[c], q, -feat_q).astype(BF16)
        qa_ref[4 + c] = qc
        q2 = qc.astype(F32)
        qn.append(jnp.max(jnp.sum(q2 * q2, axis=1, keepdims=True), axis=0, keepdims=True))

    near = jnp.where(qi > 0, qi - 1, 0)
    n_off = n - 1

    def tile_of(u, lo):
        o = jnp.where(u == 1, near, lo + (u - 2) + (lo + (u - 2) >= near).astype(jnp.int32))
        return jnp.where(u == 0, qi, o + (o >= qi).astype(jnp.int32))

    def scores_to(slot, u, lo):
        j = tile_of(u, lo)
        qsel = jnp.where(j == qi, 4, jnp.where(j < qi, 0, 2))
        r0 = pl.multiple_of(j * t, t)
        for c, hf in _ATT_CHAINS:
            qa = qa_ref[qsel + c, hf * half:(hf + 1) * half, :]
            slot[c, :, hf * half:(hf + 1) * half] = lax.dot_general(ka_ref[c, pl.ds(r0, t), :], qa, _NT,
                                                                      preferred_element_type=F32)

    def consume(slot, u, lo, diagonal=False):
        j = tile_of(u, lo)
        vt = vt_ref[:, pl.ds(pl.multiple_of(j * t, t), t)]
        if diagonal:
            colterm = jnp.zeros((1, t), F32)
        else:
            sign = jnp.where(j < qi, -slope2, slope2)
            col = lax.broadcasted_iota(jnp.int32, (1, t), 1) + (qi - j) * t
            colterm = col.astype(F32) * sign
        for c, hf in _ATT_CHAINS:
            cols = slice(hf * half, (hf + 1) * half)
            s = slot[c, :, cols]
            if diagonal:
                s = s + e_ref[:, cols]
            _attn_chain(s, colterm[:, cols], vt, m_ref, acc_ref, c, cols)

    i32 = jnp.int32
    zero = i32(0)
    scores_to(s0_ref, i32(0), zero)
    if n > 1:
        scores_to(s1_ref, i32(1), zero)
    consume(s0_ref, i32(0), zero, diagonal=True)
    if n > 1:
        reach = jnp.zeros((1, 1), F32)
        for c in range(2):
            m_min = jnp.min(m_ref[c], axis=1, keepdims=True)
            x = jnp.sqrt(qn[c] * kn_ref[c:c + 1, 0:1]) * 1.001 - m_min + UNDERFLOW
            reach = jnp.maximum(reach, x)
        d_f = jnp.floor(reach / (slope2 * t) - (1.0 / t)) + 1.0
        d_max = jnp.clip(d_f, 0.0, float(n)).astype(jnp.int32)[0, 0]
        lo = jnp.maximum(qi - d_max, 0)
        hi = jnp.minimum(qi + d_max - 1, n_off - 1)
        count = jnp.maximum(hi - lo + 1, 0)
        trips = jnp.maximum(count - 1, 0) // 2

        def pair(p, carry):
            u = 1 + 2 * p
            scores_to(s0_ref, u + 1, lo)
            consume(s1_ref, u, lo)
            scores_to(s1_ref, u + 2, lo)
            consume(s0_ref, u + 1, lo)
            return carry

        lax.fori_loop(0, trips, pair, 0)
        rem = count - 2 * trips

        @pl.when(rem == 1)
        def _():
            consume(s1_ref, count, lo)

        @pl.when(rem == 2)
        def _():
            scores_to(s0_ref, count, lo)
            consume(s1_ref, count - 1, lo)
            consume(s0_ref, count, lo)

    lam = sc_ref[DA_HEADS]
    a0, a1 = acc_ref[0], acc_ref[1]
    o = a0[:DA_DV] / a0[DA_DV:DA_DV + 1] - lam * (a1[:DA_DV] / a1[DA_DV:DA_DV + 1])
    o = o * lax.rsqrt(jnp.mean(o * o, axis=0, keepdims=True) + EPS) * (g_ref[...] * out_scale)
    o_ref[...] = o.T.astype(o_ref.dtype)


def _diff_attn(proj3, vt, scalars, g_da_col, t, out_scale):
    B, S, _ = proj3.shape
    n = S // t
    return pl.pallas_call(
        functools.partial(_attn_kernel, t=t, n=n, out_scale=out_scale),
        grid_spec=pltpu.PrefetchScalarGridSpec(
            num_scalar_prefetch=1,
            grid=(B, DA_HEADS, n),
            in_specs=[pl.BlockSpec((None, t, LANES), lambda b, h, i, sc: (b, i, CB_DQ + h)),
                      pl.BlockSpec((None, S, LANES), lambda b, h, i, sc: (b, 0, CB_DK + h)),
                      pl.BlockSpec((VT_ROWS, S), lambda b, h, i, sc: (h, b)),
                      pl.BlockSpec((DA_DV, 1), lambda b, h, i, sc: (0, 0))],
            out_specs=pl.BlockSpec((None, t, LANES), lambda b, h, i, sc: (b, i, h)),
            scratch_shapes=[pltpu.VMEM((6, t, LANES), BF16), pltpu.VMEM((2, S, LANES), BF16),
                            pltpu.VMEM((t, t), F32), pltpu.VMEM((2, LANES), F32),
                            pltpu.VMEM((2, t, t), F32), pltpu.VMEM((2, t, t), F32),
                            pltpu.VMEM((2, 1, t), F32), pltpu.VMEM((2, VT_ROWS, t), F32)]),
        out_shape=jax.ShapeDtypeStruct((B, S, DA_WIDTH), BF16),
        compiler_params=_cparams(("parallel", "parallel", "arbitrary")),
        name="diff_attn",
    )(scalars, proj3, proj3, vt, g_da_col)


def _memkv_kernel(m_ref, g_ref, w_ref, o_ref):
    h = _rms(m_ref[...], g_ref[...]).astype(BF16)
    o_ref[...] = jnp.dot(h, w_ref[...], preferred_element_type=F32).astype(BF16)


def _memkv(mem2d, g_mem, w_ckv_bf):
    T = mem2d.shape[0]
    return pl.pallas_call(
        _memkv_kernel,
        grid=(T // N_MEM,),
        in_specs=[pl.BlockSpec((N_MEM, D_MODEL), lambda i: (i, 0)),
                  pl.BlockSpec((1, D_MODEL), lambda i: (0, 0)),
                  pl.BlockSpec((D_MODEL, 2 * D_MODEL), lambda i: (0, 0))],
        out_specs=pl.BlockSpec((N_MEM, 2 * D_MODEL), lambda i: (i, 0)),
        out_shape=jax.ShapeDtypeStruct((T, 2 * D_MODEL), BF16),
        compiler_params=_cparams(("parallel",)),
        name="mem_kv",
    )(mem2d, g_mem, w_ckv_bf)


def _post_kernel(x_ref, of_ref, ob_ref, hgate_ref, oda_ref, kv_ref, ghg_ref, wout_ref, gca_ref, wcq_ref, wco_ref,
                 gffn_ref, wr_ref, br_ref,
                 x2_ref, hn_ref, ri_ref, rw_ref, cnt_ref, carry_ref, *, tm):
    @pl.when(pl.program_id(0) == 0)
    def _():
        carry_ref[...] = jnp.zeros_like(carry_ref)

    hm = tm // 2
    halves = [slice(0, hm), slice(hm, tm)]
    f32dot = functools.partial(jnp.dot, preferred_element_type=F32)

    ghg = ghg_ref[...]
    mix_hg = []
    for r in halves:
        o_hg = of_ref[r, :].astype(F32) + ob_ref[r, :].astype(F32)
        parts = [_rms(o_hg[:, i * HG_D:(i + 1) * HG_D], ghg[:, i * HG_D:(i + 1) * HG_D]) for i in range(HG_HEADS)]
        mix_hg.append((jnp.concatenate(parts, axis=-1) * _silu(hgate_ref[r, :].astype(F32))).astype(BF16))
    x1 = [x_ref[r, :] + f32dot(mh, wout_ref[:HG_WIDTH, :]) + f32dot(oda_ref[r, :], wout_ref[HG_WIDTH:, :])
          for r, mh in zip(halves, mix_hg)]

    hq_in = [_rms(v, gca_ref[...]).astype(BF16) for v in x1]
    hq = [f32dot(v, wcq_ref[...]).astype(BF16) for v in hq_in]
    scores = [[lax.dot_general(v[:, i * CA_DH:(i + 1) * CA_DH], kv_ref[:, i * CA_DH:(i + 1) * CA_DH], _NT,
                               preferred_element_type=F32) * (CA_DH ** -0.5) for i in range(CA_HEADS)] for v in hq]
    probs = []
    for sc in scores:
        ps = []
        for s_ in sc:
            e = jnp.exp(s_ - jnp.max(s_, axis=-1, keepdims=True))
            ps.append((e / jnp.sum(e, axis=-1, keepdims=True)).astype(BF16))
        probs.append(ps)
    ca = [jnp.concatenate([f32dot(ps[i], kv_ref[:, D_MODEL + i * CA_DH:D_MODEL + (i + 1) * CA_DH]).astype(BF16)
                           for i in range(CA_HEADS)], axis=-1) for ps in probs]
    x2 = [v + f32dot(c_, wco_ref[...]) for v, c_ in zip(x1, ca)]

    hn = [_rms(v, gffn_ref[...]) for v in x2]
    for r, v2, vn in zip(halves, x2, hn):
        x2_ref[r, :] = v2
        hn_ref[r, :] = _pack_bf16_pairs(vn)
    logits = []
    for vn in hn:
        hi = vn.astype(BF16)
        lo = (vn - hi.astype(F32)).astype(BF16)
        logits.append(f32dot(hi, wr_ref[0]) + f32dot(lo, wr_ref[0]) + f32dot(hi, wr_ref[1]) + br_ref[...])
    lane = lax.broadcasted_iota(jnp.int32, (hm, LANES), 1)
    big = jnp.int32(1 << 20)
    neg = jnp.float32(-jnp.inf)

    def first_max(vals):
        m = jnp.max(vals, axis=-1, keepdims=True)
        return m, jnp.min(jnp.where(vals == m, lane, big), axis=-1, keepdims=True)

    routed = []
    for lg in logits:
        gl = jnp.where(lane < N_GROUPS, lg, neg)
        gmax, gidx = first_max(gl)
        g_val = 1.0 / jnp.sum(jnp.exp(gl - gmax), axis=-1, keepdims=True)
        e_lane = lane - N_GROUPS
        in_grp = (e_lane >= 0) & (e_lane < N_EXPERTS) & ((e_lane // EXP_PER_GROUP) == gidx)
        el = jnp.where(in_grp, lg, neg)
        m1, i1 = first_max(el)
        m2, i2 = first_max(jnp.where(lane == i1, neg, el))
        t = jnp.exp(m2 - m1)
        routed.append((i1, i2, g_val / (1.0 + t), g_val * t / (1.0 + t)))

    r_i = lax.broadcasted_iota(jnp.int32, (hm, hm), 0)
    c_i = lax.broadcasted_iota(jnp.int32, (hm, hm), 1)
    prefix = (c_i < r_i).astype(BF16)
    carry = carry_ref[...]
    for r, (i1, i2, w1, w2) in zip(halves, routed):
        hit1, hit2 = lane == i1, lane == i2
        onehot = (hit1 | hit2).astype(BF16)
        before = f32dot(prefix, onehot) + carry
        r1 = jnp.sum(jnp.where(hit1, before, 0.0), axis=-1, keepdims=True).astype(jnp.int32)
        r2 = jnp.sum(jnp.where(hit2, before, 0.0), axis=-1, keepdims=True).astype(jnp.int32)
        carry = carry + jnp.sum(onehot.astype(F32), axis=0, keepdims=True)
        ri_ref[r, :] = jnp.where(lane == 0, i1 - N_GROUPS, jnp.where(lane == 1, i2 - N_GROUPS,
                                 jnp.where(lane == 2, r1, jnp.where(lane == 3, r2, 0))))
        rw_ref[r, :] = jnp.where(lane == 0, w1, jnp.where(lane == 1, w2, 0.0))
    carry_ref[...] = carry
    cnt_ref[...] = carry


def _post(x2d, o_f, o_b, proj, o_da, kv3, S, w, tm):
    T = x2d.shape[0]
    per_b = S // tm
    row = lambda i: (i, 0)
    const = lambda i: (0, 0)
    return pl.pallas_call(
        functools.partial(_post_kernel, tm=tm),
        grid=(T // tm,),
        in_specs=[pl.BlockSpec((tm, D_MODEL), row),
                  pl.BlockSpec((tm, HG_WIDTH), row),
                  pl.BlockSpec((tm, HG_WIDTH), row),
                  pl.BlockSpec((tm, HG_WIDTH), lambda i: (i, CW_HGATE)),
                  pl.BlockSpec((tm, DA_WIDTH), row),
                  pl.BlockSpec((None, N_MEM, 2 * D_MODEL), lambda i: (i // per_b, 0, 0)),
                  pl.BlockSpec((1, HG_WIDTH), const),
                  pl.BlockSpec((D_MODEL, D_MODEL), const),
                  pl.BlockSpec((1, D_MODEL), const),
                  pl.BlockSpec((D_MODEL, D_MODEL), const),
                  pl.BlockSpec((D_MODEL, D_MODEL), const),
                  pl.BlockSpec((1, D_MODEL), const),
                  pl.BlockSpec((2, D_MODEL, LANES), lambda i: (0, 0, 0)),
                  pl.BlockSpec((1, LANES), const)],
        out_specs=[pl.BlockSpec((tm, D_MODEL), row),
                   pl.BlockSpec((tm, D_PACK), row),
                   pl.BlockSpec((tm, LANES), row),
                   pl.BlockSpec((tm, LANES), row),
                   pl.BlockSpec((1, LANES), const)],
        out_shape=[jax.ShapeDtypeStruct((T, D_MODEL), F32),
                   jax.ShapeDtypeStruct((T, D_PACK), jnp.uint32),
                   jax.ShapeDtypeStruct((T, LANES), jnp.int32),
                   jax.ShapeDtypeStruct((T, LANES), F32),
                   jax.ShapeDtypeStruct((1, LANES), F32)],
        scratch_shapes=[pltpu.VMEM((1, LANES), F32)],
        compiler_params=_cparams(("arbitrary",)),
        name="post_mixer",
    )(x2d, o_f, o_b, proj, o_da, kv3, w["g_hg"], w["w_out"], w["g_ca"], w["w_cq"], w["w_co"], w["g_ffn"],
      w["w_r"], w["b_r"])


GATHER_SLOTS = 3


def _expert_kernel(be_ref, nu_ref, src_ref, src_prev_ref, tok_ref, tok_next_ref, tok_next2_ref, hn_ref, wg_ref, wu_ref,
                   wd_ref, yt_ref, xbuf_ref, ybuf_ref, zbuf_ref, gsem, ssem, *, n_tok):
    del be_ref
    i = pl.program_id(0)
    n_used = nu_ref[0]
    gslot = i % GATHER_SLOTS
    yslot = i % 2

    def gather(tok, buf_slot, r):
        return pltpu.make_async_copy(hn_ref.at[pl.ds(tok[0, r], 1), :], xbuf_ref.at[buf_slot, pl.ds(r, 1), :],
                                     gsem.at[buf_slot])

    def scatter(src, buf_slot, r):
        return pltpu.make_async_copy(ybuf_ref.at[buf_slot, pl.ds(r, 1), :], yt_ref.at[pl.ds(src[0, r], 1), :],
                                     ssem.at[buf_slot])

    def zero_fill(k):
        return pltpu.make_async_copy(zbuf_ref, yt_ref.at[pl.ds(2 * n_tok + k * MOE_BLOCK, MOE_BLOCK), :], ssem.at[0])

    def for_rows(fn):
        def body(r, c):
            fn(r)
            return c
        lax.fori_loop(0, MOE_BLOCK, body, 0, unroll=8)

    @pl.when((i == 0) & (n_used > 0))
    def _():
        ybuf_ref[...] = jnp.zeros_like(ybuf_ref)
        zbuf_ref[...] = jnp.zeros_like(zbuf_ref)
        n_fill = (yt_ref.shape[0] - 2 * n_tok) // MOE_BLOCK
        for k in range(n_fill):
            zero_fill(k).start()
        for k in range(n_fill):
            zero_fill(k).wait()
        for_rows(lambda r: gather(tok_ref, 0, r).start())
        for_rows(lambda r: gather(tok_next_ref, 1, r).start())

    @pl.when(i < n_used)
    def _():
        for_rows(lambda r: gather(tok_ref, gslot, r).wait())
        for r in range(MOE_BLOCK):
            gather(tok_next2_ref, (i + 2) % GATHER_SLOTS, r).start(priority=r % 2)
            scatter(src_prev_ref, 1 - yslot, r).start(priority=(r + 1) % 2)
        xb = _unpack_bf16_pairs(xbuf_ref[gslot]).astype(BF16)
        a = jnp.dot(xb, wg_ref[...], preferred_element_type=F32)
        u = jnp.dot(xb, wu_ref[...], preferred_element_type=F32)
        hmid = (_silu(a) * u).astype(BF16)
        y = _pack_bf16_pairs(jnp.dot(hmid, wd_ref[...], preferred_element_type=F32))

        @pl.when(i > 0)
        def _():
            for_rows(lambda r: scatter(src_ref, yslot, r).wait())

        ybuf_ref[yslot] = y

        @pl.when(i == n_used - 1)
        def _():
            for_rows(lambda r: scatter(src_ref, yslot, r).start())
            for_rows(lambda r: scatter(src_ref, yslot, r).wait())
            for_rows(lambda r: scatter(src_ref, 1 - yslot, r).wait())
            for_rows(lambda r: gather(tok_ref, (i + 1) % GATHER_SLOTS, r).wait())
            for_rows(lambda r: gather(tok_ref, (i + 2) % GATHER_SLOTS, r).wait())


def _experts(hn, slot_src, block_expert, n_used, w_gate, w_up, w_down):
    n_tok = hn.shape[0]
    n_slots = slot_src.shape[0]
    nb = n_slots // MOE_BLOCK
    lead = 2 * n_tok + jnp.arange(MOE_BLOCK, dtype=jnp.int32)
    src3 = jnp.concatenate([lead, slot_src]).reshape(nb + 1, 1, MOE_BLOCK)
    tok3 = jnp.where(src3 < n_tok, src3, jnp.where(src3 < 2 * n_tok, src3 - n_tok, n_tok - 1))
    n_rows = n_slots + MOE_BLOCK

    def src_spec(off):
        return pl.BlockSpec((None, 1, MOE_BLOCK), lambda i, be, nu: (jnp.minimum(i + 1 + off, nb), 0, 0),
                            memory_space=pltpu.SMEM)

    return pl.pallas_call(
        functools.partial(_expert_kernel, n_tok=n_tok),
        grid_spec=pltpu.PrefetchScalarGridSpec(
            num_scalar_prefetch=2,
            grid=(nb,),
            in_specs=[src_spec(0), src_spec(-1), src_spec(0), src_spec(1), src_spec(2),
                      pl.BlockSpec(memory_space=pl.ANY),
                      pl.BlockSpec((None, D_MODEL, D_EXPERT), lambda i, be, nu: (be[i], 0, 0)),
                      pl.BlockSpec((None, D_MODEL, D_EXPERT), lambda i, be, nu: (be[i], 0, 0)),
                      pl.BlockSpec((None, D_EXPERT, D_MODEL), lambda i, be, nu: (be[i], 0, 0))],
            out_specs=pl.BlockSpec(memory_space=pl.ANY),
            scratch_shapes=[pltpu.VMEM((GATHER_SLOTS, MOE_BLOCK, D_PACK), jnp.uint32),
                            pltpu.VMEM((2, MOE_BLOCK, D_PACK), jnp.uint32),
                            pltpu.VMEM((MOE_BLOCK, D_PACK), jnp.uint32),
                            pltpu.SemaphoreType.DMA((GATHER_SLOTS,)), pltpu.SemaphoreType.DMA((2,))]),
        out_shape=jax.ShapeDtypeStruct((n_rows, D_PACK), jnp.uint32),
        compiler_params=_cparams(("arbitrary",), has_side_effects=True),
        name="moe_experts",
    )(block_expert, n_used, src3, src3, tok3, tok3, tok3, hn, w_gate, w_up, w_down)


def _combine_kernel(x2_ref, rw_ref, g_ref, y1_ref, y2_ref, y_ref):
    rw = rw_ref[...]
    lane = lax.broadcasted_iota(jnp.int32, rw.shape, 1)
    w1 = jnp.sum(jnp.where(lane == 0, rw, 0.0), axis=-1, keepdims=True)
    w2 = jnp.sum(jnp.where(lane == 1, rw, 0.0), axis=-1, keepdims=True)
    x3 = x2_ref[...] + (_unpack_bf16_pairs(y1_ref[...]) * w1 + _unpack_bf16_pairs(y2_ref[...]) * w2)
    y_ref[...] = _rms(x3, g_ref[...])


def _combine(x2, rw, g_final, yt, tm):
    T = x2.shape[0]
    return pl.pallas_call(
        _combine_kernel,
        grid=(T // tm,),
        in_specs=[pl.BlockSpec((tm, D_MODEL), lambda i: (i, 0)),
                  pl.BlockSpec((tm, LANES), lambda i: (i, 0)),
                  pl.BlockSpec((1, D_MODEL), lambda i: (0, 0)),
                  pl.BlockSpec((tm, D_PACK), lambda i: (i, 0)),
                  pl.BlockSpec((tm, D_PACK), lambda i: (i + T // tm, 0))],
        out_specs=pl.BlockSpec((tm, D_MODEL), lambda i: (i, 0)),
        out_shape=jax.ShapeDtypeStruct((T, D_MODEL), F32),
        compiler_params=_cparams(("parallel",)),
        name="moe_combine",
    )(x2, rw, g_final, yt, yt)


def _tile(n, pref):
    return min(n, pref)


def _trunk(x, mem, w):
    B, S, _ = x.shape
    T = B * S
    x2d = x.reshape(T, D_MODEL)
    proj, vt = _inproj(x2d, w["g_mix"], w["w_in"], w["w_vt"], _tile(T, PROJ_TM))
    proj3 = proj.reshape(B, S, IN_COLS)
    o_f, o_b = _hgrn(proj3, w["lb"], _tile(S, HG_ROWS))
    o_da = _diff_attn(proj3, vt, w["attn_scalars"], w["g_da"], _tile(S, ATT_T), w["da_out_scale"])
    kv = _memkv(mem.reshape(B * N_MEM, D_MODEL), w["g_mem"], w["w_ckv"]).reshape(B, N_MEM, 2 * D_MODEL)
    tm = _tile(S, POST_TM)
    x2, hn, ri, rw, cnt = _post(x2d, o_f.reshape(T, HG_WIDTH), o_b.reshape(T, HG_WIDTH), proj,
                                o_da.reshape(T, DA_WIDTH), kv, S, w, tm)

    counts = cnt[0, N_GROUPS:N_GROUPS + N_EXPERTS].astype(jnp.int32)
    padded = (counts + MOE_BLOCK - 1) // MOE_BLOCK * MOE_BLOCK
    pad_end = jnp.cumsum(padded)
    pad_start = pad_end - padded
    dest = pad_start[ri[:, 0:2]] + ri[:, 2:4]
    n_slots = 2 * T + N_EXPERTS * MOE_BLOCK
    nb = n_slots // MOE_BLOCK
    block_start = jnp.arange(nb, dtype=jnp.int32) * MOE_BLOCK
    block_expert = jnp.minimum(jnp.sum((pad_end[None, :] <= block_start[:, None]).astype(jnp.int32), axis=1),
                               N_EXPERTS - 1)
    n_used = (pad_end[-1:] // MOE_BLOCK).astype(jnp.int32)
    e_slot = jnp.repeat(block_expert, MOE_BLOCK)
    pad_before = jnp.cumsum(padded - counts) - (padded - counts)
    pad_rank = pad_before[e_slot] + jnp.arange(n_slots, dtype=jnp.int32) - pad_start[e_slot] - counts[e_slot]
    assign = jnp.arange(2 * T, dtype=jnp.int32)
    slot_src = (2 * T + MOE_BLOCK + pad_rank).at[dest.reshape(-1)].set((assign & 1) * T + (assign >> 1))
    yt = _experts(hn, slot_src, block_expert, n_used, w["w_gate"], w["w_up"], w["w_down"])
    y = _combine(x2, rw, w["g_final"], yt, tm)
    return y.reshape(B, S, D_MODEL)


def _split_bf16(a):
    hi = a.astype(BF16)
    return jnp.stack([hi, (a - hi.astype(F32)).astype(BF16)])


def kernel(x_prompt, x_sample, mem_prompt, mem_sample, g_mix, w_in, hg_lb, g_hg, lam_q1, lam_k1, lam_q2, lam_k2,
           g_da, w_out, g_ca, g_mem, w_cq, w_ckv, w_co, g_ffn, w_rg, b_rg, w_re, b_re, w_gate, w_up, w_down, g_final):
    l = 0
    lam_init = 0.8 - 0.6 * math.exp(-0.3 * l)
    lam = (jnp.exp(jnp.sum(lam_q1[l] * lam_k1[l])) - jnp.exp(jnp.sum(lam_q2[l] * lam_k2[l])) + lam_init)
    slopes = jnp.power(2.0, -8.0 * jnp.arange(1, DA_HEADS + 1, dtype=F32) / DA_HEADS)
    pad = LANES - N_GROUPS - N_EXPERTS
    w = {
        "g_mix": g_mix[l][None], "w_in": w_in[l].astype(BF16),
        "w_vt": w_in[l][:, CB_DV * LANES:].T.astype(BF16),
        "lb": jnp.cumsum(jax.nn.softmax(hg_lb, axis=0), axis=0)[l],
        "g_hg": g_hg[l][None],
        "attn_scalars": jnp.concatenate([slopes, lam[None]]).astype(F32),
        "g_da": g_da[l][:, None], "da_out_scale": 1.0 - lam_init,
        "w_out": w_out[l].astype(BF16), "g_ca": g_ca[l][None], "g_mem": g_mem[l][None],
        "w_cq": w_cq[l].astype(BF16), "w_ckv": w_ckv[l].astype(BF16), "w_co": w_co[l].astype(BF16),
        "g_ffn": g_ffn[l][None],
        "w_r": _split_bf16(jnp.pad(jnp.concatenate([w_rg[l], w_re[l]], axis=1), ((0, 0), (0, pad)))),
        "b_r": jnp.pad(jnp.concatenate([b_rg[l], b_re[l]]), (0, pad))[None],
        "w_gate": w_gate[l].astype(BF16), "w_up": w_up[l].astype(BF16), "w_down": w_down[l].astype(BF16),
        "g_final": g_final[None],
    }
    return (_trunk(x_prompt, mem_prompt, w), _trunk(x_sample, mem_sample, w))
```

```python
import functools
import math

import jax
import jax.numpy as jnp
from jax import lax
from jax.experimental import pallas as pl
from jax.experimental.pallas import tpu as pltpu

F32 = jnp.float32
BF16 = jnp.bfloat16

D_MODEL = 1024
N_MEM = 256
EPS = 1e-6
HG_HEADS = 4
HG_D = 128
HG_WIDTH = HG_HEADS * HG_D
DA_HEADS = 4
DA_DK = 64
DA_DV = 128
DA_WIDTH = DA_HEADS * DA_DV
IN_COLS = 4096
CA_HEADS = 4
CA_DH = 256
N_GROUPS = 4
EXP_PER_GROUP = 8
N_EXPERTS = 32
D_EXPERT = 512
LANES = 128
D_PACK = D_MODEL // 2
VT_ROWS = DA_DV + 16

CW_HQ, CW_FF, CW_FB, CW_HI, CW_HGATE = 0, 1, 2, 3, 4
CB_DQ, CB_DK, CB_DV = 20, 24, 28

HG_CHUNK = 64
HG_SUB = 16
EXP_CLAMP = 80.0
MOE_BLOCK = 256

VMEM_LIMIT = 48 * 1024 * 1024

PROJ_TM = 512
HG_ROWS = 256
ATT_T = 512
POST_TM = 512


def _cparams(sem, **kw):
    return pltpu.CompilerParams(dimension_semantics=sem, vmem_limit_bytes=VMEM_LIMIT, **kw)


def _rms(x, g):
    return x * lax.rsqrt(jnp.mean(x * x, axis=-1, keepdims=True) + EPS) * g


def _silu(x):
    return x * (1.0 / (1.0 + jnp.exp(-x)))


def _pack_bf16_pairs(x):
    n = x.shape[1] // 2
    lo = lax.bitcast_convert_type(x[:, :n].astype(BF16).astype(F32), jnp.uint32)
    hi = lax.bitcast_convert_type(x[:, n:].astype(BF16).astype(F32), jnp.uint32)
    return (lo >> 16) | (hi & jnp.uint32(0xFFFF0000))


def _unpack_bf16_pairs(w):
    lo = lax.bitcast_convert_type(w << 16, F32)
    hi = lax.bitcast_convert_type(w & jnp.uint32(0xFFFF0000), F32)
    return jnp.concatenate([lo, hi], axis=1)


def _inproj_kernel(x_ref, g_ref, w_ref, wvt_ref, o_ref, vt_ref, *, n_chunk):
    h = _rms(x_ref[...], g_ref[...]).astype(BF16)
    for j in range(IN_COLS // n_chunk):
        sl = slice(j * n_chunk, (j + 1) * n_chunk)
        o_ref[:, sl] = jnp.dot(h, w_ref[:, sl], preferred_element_type=F32).astype(BF16)
    vt = lax.dot_general(wvt_ref[...], h, (((1,), (1,)), ((), ())), preferred_element_type=F32).astype(BF16)
    for hd in range(DA_HEADS):
        vt_ref[hd * VT_ROWS:hd * VT_ROWS + DA_DV, :] = vt[hd * DA_DV:(hd + 1) * DA_DV]
        vt_ref[hd * VT_ROWS + DA_DV:(hd + 1) * VT_ROWS, :] = jnp.ones((VT_ROWS - DA_DV, vt.shape[1]), BF16)


def _inproj(x2d, g_mix, w_in_bf, w_vt_bf, tm):
    T = x2d.shape[0]
    return pl.pallas_call(
        functools.partial(_inproj_kernel, n_chunk=512),
        grid=(T // tm,),
        in_specs=[pl.BlockSpec((tm, D_MODEL), lambda i: (i, 0)),
                  pl.BlockSpec((1, D_MODEL), lambda i: (0, 0)),
                  pl.BlockSpec((D_MODEL, IN_COLS), lambda i: (0, 0)),
                  pl.BlockSpec((DA_WIDTH, D_MODEL), lambda i: (0, 0))],
        out_specs=[pl.BlockSpec((tm, IN_COLS), lambda i: (i, 0)),
                   pl.BlockSpec((DA_HEADS * VT_ROWS, tm), lambda i: (0, i))],
        out_shape=[jax.ShapeDtypeStruct((T, IN_COLS), BF16), jax.ShapeDtypeStruct((DA_HEADS * VT_ROWS, T), BF16)],
        compiler_params=_cparams(("parallel",)),
        name="inproj",
    )(x2d, g_mix, w_in_bf, w_vt_bf)


def _sigmoid(x):
    return 0.5 + 0.5 * jnp.tanh(0.5 * x)


def _hgrn_prep(dirs, lb, tri2):
    C, S = HG_CHUNK, HG_SUB
    n_sub = C // S
    stage1 = []
    for d, (q_raw, f_raw, v) in enumerate(dirs):
        xq = q_raw.astype(F32)
        q = xq * _sigmoid(xq)
        f = lb + (1.0 - lb) * _sigmoid(f_raw.astype(F32))
        lf = jnp.log(f)
        hi = lf.astype(BF16)
        mid = (lf - hi.astype(F32)).astype(BF16)
        cc = jnp.dot(tri2[d], hi, preferred_element_type=F32) + jnp.dot(tri2[d], mid, preferred_element_type=F32)
        stage1.append((q, 1.0 - f, v, cc[:C], cc[C:]))
    stage2 = []
    for d, (q, k, v, cum, cum_loc) in enumerate(stage1):
        ref_pt = cum - cum_loc
        edge = cum[0:1, :] if d == 1 else cum[C - 1:C, :]
        q_inter = (q * jnp.exp(cum)).astype(BF16)
        q_loc = (q * jnp.exp(cum_loc)).astype(BF16)
        k_st = (k * jnp.exp(edge - cum)).astype(BF16)
        k_sub = [(k * jnp.exp(jnp.minimum(ref_pt[i * S:i * S + 1, :] - cum, EXP_CLAMP))).astype(BF16)
                 for i in range(n_sub)]
        stage2.append((q_inter, q_loc, k_st, k_sub, jnp.exp(edge), v, q, k, cum))
    return stage2


def _hgrn_exact_scores(q, k, cum, keep_d, xq_ref, xk_ref, xc_ref):
    C = HG_CHUNK
    xq_ref[...] = q
    xk_ref[...] = k
    xc_ref[...] = cum
    lane = lax.broadcasted_iota(jnp.int32, (C, C), 1)

    def body(s_, accs):
        w = (jnp.exp(jnp.minimum(xc_ref[...] - xc_ref[pl.ds(s_, 1), :], 0.0)) * xq_ref[...]) * xk_ref[pl.ds(s_, 1), :]
        return tuple(jnp.where(lane == s_, jnp.sum(w[:, h * HG_D:(h + 1) * HG_D], axis=1, keepdims=True), accs[h])
                     for h in range(HG_HEADS))

    accs = lax.fori_loop(0, C, body, tuple(jnp.zeros((C, C), F32) for _ in range(HG_HEADS)))
    return [jnp.where(keep_d, a, 0.0).astype(BF16) for a in accs]


def _hgrn_apply(stage2, st_refs, keep, exact_refs=None):
    C, S = HG_CHUNK, HG_SUB
    n_sub = C // S
    nt = (((1,), (1,)), ((), ()))
    chains = [(d, h) for d in range(2) for h in range(HG_HEADS)]
    hs = lambda h: slice(h * HG_D, (h + 1) * HG_D)
    st_old, o_inter, scores = {}, {}, {}
    if exact_refs is not None:
        for d in range(2):
            ex = _hgrn_exact_scores(stage2[d][6], stage2[d][7], stage2[d][8], keep[d], *exact_refs)
            for h in range(HG_HEADS):
                scores[d, h] = ex[h]
    for d, h in chains:
        q_inter, q_loc, _, k_sub = stage2[d][:4]
        st_old[d, h] = st_refs[d][h]
        o_inter[d, h] = lax.dot_general(q_inter[:, hs(h)], st_old[d, h].astype(BF16), nt, preferred_element_type=F32)
        if exact_refs is None:
            sc = jnp.concatenate([lax.dot_general(q_loc[i * S:(i + 1) * S, hs(h)], k_sub[i][:, hs(h)], nt,
                                                  preferred_element_type=F32) for i in range(n_sub)], axis=0)
            scores[d, h] = jnp.where(keep[d], sc, 0.0).astype(BF16)
    outs = {}
    for d, h in chains:
        outs[d, h] = o_inter[d, h] + jnp.dot(scores[d, h], stage2[d][5][:, hs(h)], preferred_element_type=F32)
    for d, h in chains:
        k_st, dec, v = stage2[d][2], stage2[d][4], stage2[d][5]
        st_refs[d][h] = st_old[d, h] * dec[:, hs(h)] + lax.dot_general(
            v[:, hs(h)], k_st[:, hs(h)], (((0,), (0,)), ((), ())), preferred_element_type=F32)
    return [jnp.concatenate([outs[d, h] for h in range(HG_HEADS)], axis=-1) for d in range(2)]


def _hgrn_kernel(lb_ref, qf_ref, ff_ref, vf_ref, qb_ref, fb_ref, vb_ref, of_ref, ob_ref, stf_ref, stb_ref,
                 xq_ref, xk_ref, xc_ref, *, rows):
    @pl.when(pl.program_id(1) == 0)
    def _():
        stf_ref[...] = jnp.zeros_like(stf_ref)
        stb_ref[...] = jnp.zeros_like(stb_ref)

    C, S = HG_CHUNK, HG_SUB
    r = lax.broadcasted_iota(jnp.int32, (2 * C, C), 0)
    c = lax.broadcasted_iota(jnp.int32, (2 * C, C), 1)
    rr = jnp.where(r >= C, r - C, r)
    local = (r < C) | ((rr // S) == (c // S))
    tri2 = [jnp.where((c <= rr) & local, 1.0, 0.0).astype(BF16), jnp.where((c >= rr) & local, 1.0, 0.0).astype(BF16)]
    r1 = lax.broadcasted_iota(jnp.int32, (C, C), 0)
    c1 = lax.broadcasted_iota(jnp.int32, (C, C), 1)
    keep = [c1 <= r1, c1 >= r1]
    lb = lb_ref[...]
    n = rows // C

    def trip(exact_refs, jj):
        offs = []
        for j in (2 * jj, 2 * jj + 1):
            offs.append((pl.multiple_of(j * C, C), pl.multiple_of((n - 1 - j) * C, C)))
        preps = [_hgrn_prep([(qf_ref[pl.ds(a, C), :], ff_ref[pl.ds(a, C), :], vf_ref[pl.ds(a, C), :]),
                             (qb_ref[pl.ds(b, C), :], fb_ref[pl.ds(b, C), :], vb_ref[pl.ds(b, C), :])], lb, tri2)
                 for a, b in offs]
        for (a, b), st2 in zip(offs, preps):
            o_f, o_b = _hgrn_apply(st2, (stf_ref, stb_ref), keep, exact_refs)
            of_ref[pl.ds(a, C), :] = o_f.astype(of_ref.dtype)
            ob_ref[pl.ds(b, C), :] = o_b.astype(ob_ref.dtype)

    def run(exact_refs):
        def body(jj, carry):
            trip(exact_refs, jj)
            return carry
        lax.fori_loop(0, n // 2, body, 0)

    x_min = jnp.minimum(jnp.min(ff_ref[...].astype(F32), axis=0, keepdims=True),
                        jnp.min(fb_ref[...].astype(F32), axis=0, keepdims=True))
    f_min = jnp.min(lb + (1.0 - lb) * _sigmoid(x_min), axis=1, keepdims=True)[0, 0]
    safe = f_min > math.exp(-(EXP_CLAMP - 1.0) / HG_SUB)
    pl.when(safe)(lambda: run(None))
    pl.when(jnp.logical_not(safe))(lambda: run((xq_ref, xk_ref, xc_ref)))


def _hgrn(proj3, lb, rows):
    B, S, _ = proj3.shape
    n = S // rows
    W = HG_WIDTH

    def fwd(cb):
        return pl.BlockSpec((None, rows, W), lambda b, c: (b, c, cb))

    def bwd(cb):
        return pl.BlockSpec((None, rows, W), lambda b, c: (b, n - 1 - c, cb))

    out = jax.ShapeDtypeStruct((B, S, W), BF16)
    return pl.pallas_call(
        functools.partial(_hgrn_kernel, rows=rows),
        grid=(B, n),
        in_specs=[pl.BlockSpec((1, W), lambda b, c: (0, 0)),
                  fwd(CW_HQ), fwd(CW_FF), fwd(CW_HI), bwd(CW_HQ), bwd(CW_FB), bwd(CW_HI)],
        out_specs=[pl.BlockSpec((None, rows, W), lambda b, c: (b, c, 0)),
                   pl.BlockSpec((None, rows, W), lambda b, c: (b, n - 1 - c, 0))],
        out_shape=[out, out],
        scratch_shapes=[pltpu.VMEM((HG_HEADS, HG_D, HG_D), F32), pltpu.VMEM((HG_HEADS, HG_D, HG_D), F32),
                        pltpu.VMEM((HG_CHUNK, W), F32), pltpu.VMEM((HG_CHUNK, W), F32), pltpu.VMEM((HG_CHUNK, W), F32)],
        compiler_params=_cparams(("parallel", "arbitrary")),
        name="hgrn2",
    )(lb.reshape(1, W), proj3, proj3, proj3, proj3, proj3, proj3)


LOG2E = 1.4426950408889634
N_FEAT = 6
UNDERFLOW = 160.0


def _attn_chain(s, colterm, vt, m_ref, acc_ref, c, cols):
    m_old = m_ref[c, :, cols]
    m_new = jnp.maximum(m_old, jnp.max(s, axis=0, keepdims=True) + colterm)
    alpha = jnp.exp2(m_old - m_new)
    p = jnp.exp2(s - (m_new - colterm)).astype(BF16)
    acc_ref[c, :, cols] = alpha * acc_ref[c, :, cols] + jnp.dot(vt, p, preferred_element_type=F32)
    m_ref[c, :, cols] = m_new


_ATT_CHAINS = [(c, hf) for c in range(2) for hf in range(2)]
_NT = (((1,), (1,)), ((), ()))


def _attn_kernel(sc_ref, q_ref, k_ref, vt_ref, g_ref, o_ref, qa_ref, ka_ref, e_ref, kn_ref, s0_ref, s1_ref,
                 m_ref, acc_ref, *, t, n, out_scale):
    h, qi = pl.program_id(1), pl.program_id(2)
    slope2 = sc_ref[h] * LOG2E
    half = t // 2
    lane = lax.broadcasted_iota(jnp.int32, (t, LANES), 1)
    row = lax.broadcasted_iota(jnp.int32, (t, LANES), 0)
    feat_lane = [lane - (DA_DK if c == 0 else 0) for c in range(2)]
    is_feat = [(fl >= 0) & (fl < N_FEAT) for fl in feat_lane]
    own = [(lane < DA_DK) == (c == 0) for c in range(2)]

    @pl.when(qi == 0)
    def _():
        r_hi = ((row >> 4) << 4).astype(F32)
        r_lo = (row & 15).astype(F32)
        kn = [jnp.zeros((1, 1), F32), jnp.zeros((1, 1), F32)]
        for c in range(2):
            fk = jnp.where((feat_lane[c] & 1) == 0, r_hi, r_lo).astype(BF16)
            for j in range(n):
                kj = k_ref[j * t:(j + 1) * t, :]
                ka_ref[c, j * t:(j + 1) * t, :] = jnp.where(is_feat[c], fk, kj)
                k2 = jnp.where(own[c], kj.astype(F32), 0.0)
                kn[c] = jnp.maximum(kn[c], jnp.max(jnp.sum(k2 * k2, axis=1, keepdims=True), axis=0, keepdims=True))
        kn_ref[...] = jnp.concatenate([jnp.broadcast_to(kn[0], (1, LANES)), jnp.broadcast_to(kn[1], (1, LANES))], axis=0)
        rel = lax.broadcasted_iota(jnp.int32, (t, t), 1) - lax.broadcasted_iota(jnp.int32, (t, t), 0)
        e_ref[...] = jnp.abs(rel).astype(F32) * (-slope2)

    m_ref[...] = jnp.full_like(m_ref, -jnp.inf)
    acc_ref[...] = jnp.zeros_like(acc_ref)
    q = q_ref[...].astype(F32) * (DA_DK ** -0.5 * LOG2E)
    sv = jnp.full((t, LANES), slope2, F32)
    s_hi = sv.astype(BF16).astype(F32)
    s_mid = (sv - s_hi).astype(BF16).astype(F32)
    s_lo = (sv - s_hi - s_mid).astype(BF16).astype(F32)
    qn = []
    for c in range(2):
        fl = feat_lane[c]
        feat_q = jnp.where(is_feat[c], jnp.where(fl < 2, s_hi, jnp.where(fl < 4, s_mid, s_lo)), 0.0)
        qc = jnp.where(own[c], q, 0.0).astype(BF16)
        qa_ref[0 + c] = jnp.where(own[c], q, feat_q).astype(BF16)
        qa_ref[2 + c] = jnp.where(own[c], q, -feat_q).astype(BF16)
        qa_ref[4 + c] = qc
        q2 = qc.astype(F32)
        qn.append(jnp.max(jnp.sum(q2 * q2, axis=1, keepdims=True), axis=0, keepdims=True))

    near = jnp.where(qi > 0, qi - 1, 0)
    n_off = n - 1

    def tile_of(u, lo):
        o = jnp.where(u == 1, near, lo + (u - 2) + (lo + (u - 2) >= near).astype(jnp.int32))
        return jnp.where(u == 0, qi, o + (o >= qi).astype(jnp.int32))

    def scores_to(slot, u, lo):
        j = tile_of(u, lo)
        qsel = jnp.where(j == qi, 4, jnp.where(j < qi, 0, 2))
        r0 = pl.multiple_of(j * t, t)
        for c, hf in _ATT_CHAINS:
            qa = qa_ref[qsel + c, hf * half:(hf + 1) * half, :]
            slot[c, :, hf * half:(hf + 1) * half] = lax.dot_general(ka_ref[c, pl.ds(r0, t), :], qa, _NT,
                                                                      preferred_element_type=F32)

    def consume(slot, u, lo, diagonal=False):
        j = tile_of(u, lo)
        vt = vt_ref[:, pl.ds(pl.multiple_of(j * t, t), t)]
        if diagonal:
            colterm = jnp.zeros((1, t), F32)
        else:
            sign = jnp.where(j < qi, -slope2, slope2)
            col = lax.broadcasted_iota(jnp.int32, (1, t), 1) + (qi - j) * t
            colterm = col.astype(F32) * sign
        for c, hf in _ATT_CHAINS:
            cols = slice(hf * half, (hf + 1) * half)
            s = slot[c, :, cols]
            if diagonal:
                s = s + e_ref[:, cols]
            _attn_chain(s, colterm[:, cols], vt, m_ref, acc_ref, c, cols)

    i32 = jnp.int32
    zero = i32(0)
    scores_to(s0_ref, i32(0), zero)
    if n > 1:
        scores_to(s1_ref, i32(1), zero)
    consume(s0_ref, i32(0), zero, diagonal=True)
    if n > 1:
        reach = jnp.zeros((1, 1), F32)
        for c in range(2):
            m_min = jnp.min(m_ref[c], axis=1, keepdims=True)
            x = jnp.sqrt(qn[c] * kn_ref[c:c + 1, 0:1]) * 1.001 - m_min + UNDERFLOW
            reach = jnp.maximum(reach, x)
        d_f = jnp.floor(reach / (slope2 * t) - (1.0 / t)) + 1.0
        d_max = jnp.clip(d_f, 0.0, float(n)).astype(jnp.int32)[0, 0]
        lo = jnp.maximum(qi - d_max, 0)
        hi = jnp.minimum(qi + d_max - 1, n_off - 1)
        count = jnp.maximum(hi - lo + 1, 0)
        trips = jnp.maximum(count - 1, 0) // 2

        def pair(p, carry):
            u = 1 + 2 * p
            scores_to(s0_ref, u + 1, lo)
            consume(s1_ref, u, lo)
            scores_to(s1_ref, u + 2, lo)
            consume(s0_ref, u + 1, lo)
            return carry

        lax.fori_loop(0, trips, pair, 0)
        rem = count - 2 * trips

        @pl.when(rem == 1)
        def _():
            consume(s1_ref, count, lo)

        @pl.when(rem == 2)
        def _():
            scores_to(s0_ref, count, lo)
            consume(s1_ref, count - 1, lo)
            consume(s0_ref, count, lo)

    lam = sc_ref[DA_HEADS]
    a0, a1 = acc_ref[0], acc_ref[1]
    o = a0[:DA_DV] / a0[DA_DV:DA_DV + 1] - lam * (a1[:DA_DV] / a1[DA_DV:DA_DV + 1])
    o = o * lax.rsqrt(jnp.mean(o * o, axis=0, keepdims=True) + EPS) * (g_ref[...] * out_scale)
    o_ref[...] = o.T.astype(o_ref.dtype)


def _diff_attn(proj3, vt, scalars, g_da_col, t, out_scale):
    B, S, _ = proj3.shape
    n = S // t
    return pl.pallas_call(
        functools.partial(_attn_kernel, t=t, n=n, out_scale=out_scale),
        grid_spec=pltpu.PrefetchScalarGridSpec(
            num_scalar_prefetch=1,
            grid=(B, DA_HEADS, n),
            in_specs=[pl.BlockSpec((None, t, LANES), lambda b, h, i, sc: (b, i, CB_DQ + h)),
                      pl.BlockSpec((None, S, LANES), lambda b, h, i, sc: (b, 0, CB_DK + h)),
                      pl.BlockSpec((VT_ROWS, S), lambda b, h, i, sc: (h, b)),
                      pl.BlockSpec((DA_DV, 1), lambda b, h, i, sc: (0, 0))],
            out_specs=pl.BlockSpec((None, t, LANES), lambda b, h, i, sc: (b, i, h)),
            scratch_shapes=[pltpu.VMEM((6, t, LANES), BF16), pltpu.VMEM((2, S, LANES), BF16),
                            pltpu.VMEM((t, t), F32), pltpu.VMEM((2, LANES), F32),
                            pltpu.VMEM((2, t, t), F32), pltpu.VMEM((2, t, t), F32),
                            pltpu.VMEM((2, 1, t), F32), pltpu.VMEM((2, VT_ROWS, t), F32)]),
        out_shape=jax.ShapeDtypeStruct((B, S, DA_WIDTH), BF16),
        compiler_params=_cparams(("parallel", "parallel", "arbitrary")),
        name="diff_attn",
    )(scalars, proj3, proj3, vt, g_da_col)


def _memkv_kernel(m_ref, g_ref, w_ref, o_ref):
    h = _rms(m_ref[...], g_ref[...]).astype(BF16)
    o_ref[...] = jnp.dot(h, w_ref[...], preferred_element_type=F32).astype(BF16)


def _memkv(mem2d, g_mem, w_ckv_bf):
    T = mem2d.shape[0]
    return pl.pallas_call(
        _memkv_kernel,
        grid=(T // N_MEM,),
        in_specs=[pl.BlockSpec((N_MEM, D_MODEL), lambda i: (i, 0)),
                  pl.BlockSpec((1, D_MODEL), lambda i: (0, 0)),
                  pl.BlockSpec((D_MODEL, 2 * D_MODEL), lambda i: (0, 0))],
        out_specs=pl.BlockSpec((N_MEM, 2 * D_MODEL), lambda i: (i, 0)),
        out_shape=jax.ShapeDtypeStruct((T, 2 * D_MODEL), BF16),
        compiler_params=_cparams(("parallel",)),
        name="mem_kv",
    )(mem2d, g_mem, w_ckv_bf)


def _post_kernel(x_ref, of_ref, ob_ref, hgate_ref, oda_ref, kv_ref, ghg_ref, wout_ref, gca_ref, wcq_ref, wco_ref,
                 gffn_ref, wr_ref, br_ref,
                 x2_ref, hn_ref, ri_ref, rw_ref, cnt_ref, carry_ref, *, tm):
    @pl.when(pl.program_id(0) == 0)
    def _():
        carry_ref[...] = jnp.zeros_like(carry_ref)

    hm = tm // 2
    halves = [slice(0, hm), slice(hm, tm)]
    f32dot = functools.partial(jnp.dot, preferred_element_type=F32)

    ghg = ghg_ref[...]
    mix_hg = []
    for r in halves:
        o_hg = of_ref[r, :].astype(F32) + ob_ref[r, :].astype(F32)
        parts = [_rms(o_hg[:, i * HG_D:(i + 1) * HG_D], ghg[:, i * HG_D:(i + 1) * HG_D]) for i in range(HG_HEADS)]
        mix_hg.append((jnp.concatenate(parts, axis=-1) * _silu(hgate_ref[r, :].astype(F32))).astype(BF16))
    x1 = [x_ref[r, :] + f32dot(mh, wout_ref[:HG_WIDTH, :]) + f32dot(oda_ref[r, :], wout_ref[HG_WIDTH:, :])
          for r, mh in zip(halves, mix_hg)]

    hq_in = [_rms(v, gca_ref[...]).astype(BF16) for v in x1]
    hq = [f32dot(v, wcq_ref[...]).astype(BF16) for v in hq_in]
    scores = [[lax.dot_general(v[:, i * CA_DH:(i + 1) * CA_DH], kv_ref[:, i * CA_DH:(i + 1) * CA_DH], _NT,
                               preferred_element_type=F32) * (CA_DH ** -0.5) for i in range(CA_HEADS)] for v in hq]
    probs = []
    for sc in scores:
        ps = []
        for s_ in sc:
            e = jnp.exp(s_ - jnp.max(s_, axis=-1, keepdims=True))
            ps.append((e / jnp.sum(e, axis=-1, keepdims=True)).astype(BF16))
        probs.append(ps)
    ca = [jnp.concatenate([f32dot(ps[i], kv_ref[:, D_MODEL + i * CA_DH:D_MODEL + (i + 1) * CA_DH]).astype(BF16)
                           for i in range(CA_HEADS)], axis=-1) for ps in probs]
    x2 = [v + f32dot(c_, wco_ref[...]) for v, c_ in zip(x1, ca)]

    hn = [_rms(v, gffn_ref[...]) for v in x2]
    for r, v2, vn in zip(halves, x2, hn):
        x2_ref[r, :] = v2
        hn_ref[r, :] = _pack_bf16_pairs(vn)
    logits = []
    for vn in hn:
        hi = vn.astype(BF16)
        lo = (vn - hi.astype(F32)).astype(BF16)
        logits.append(f32dot(hi, wr_ref[0]) + f32dot(lo, wr_ref[0]) + f32dot(hi, wr_ref[1]) + br_ref[...])
    lane = lax.broadcasted_iota(jnp.int32, (hm, LANES), 1)
    big = jnp.int32(1 << 20)
    neg = jnp.float32(-jnp.inf)

    def first_max(vals):
        m = jnp.max(vals, axis=-1, keepdims=True)
        return m, jnp.min(jnp.where(vals == m, lane, big), axis=-1, keepdims=True)

    routed = []
    for lg in logits:
        gl = jnp.where(lane < N_GROUPS, lg, neg)
        gmax, gidx = first_max(gl)
        g_val = 1.0 / jnp.sum(jnp.exp(gl - gmax), axis=-1, keepdims=True)
        e_lane = lane - N_GROUPS
        in_grp = (e_lane >= 0) & (e_lane < N_EXPERTS) & ((e_lane // EXP_PER_GROUP) == gidx)
        el = jnp.where(in_grp, lg, neg)
        m1, i1 = first_max(el)
        m2, i2 = first_max(jnp.where(lane == i1, neg, el))
        t = jnp.exp(m2 - m1)
        routed.append((i1, i2, g_val / (1.0 + t), g_val * t / (1.0 + t)))

    r_i = lax.broadcasted_iota(jnp.int32, (hm, hm), 0)
    c_i = lax.broadcasted_iota(jnp.int32, (hm, hm), 1)
    prefix = (c_i < r_i).astype(BF16)
    carry = carry_ref[...]
    for r, (i1, i2, w1, w2) in zip(halves, routed):
        hit1, hit2 = lane == i1, lane == i2
        onehot = (hit1 | hit2).astype(BF16)
        before = f32dot(prefix, onehot) + carry
        r1 = jnp.sum(jnp.where(hit1, before, 0.0), axis=-1, keepdims=True).astype(jnp.int32)
        r2 = jnp.sum(jnp.where(hit2, before, 0.0), axis=-1, keepdims=True).astype(jnp.int32)
        carry = carry + jnp.sum(onehot.astype(F32), axis=0, keepdims=True)
        ri_ref[r, :] = jnp.where(lane == 0, i1 - N_GROUPS, jnp.where(lane == 1, i2 - N_GROUPS,
                                 jnp.where(lane == 2, r1, jnp.where(lane == 3, r2, 0))))
        rw_ref[r, :] = jnp.where(lane == 0, w1, jnp.where(lane == 1, w2, 0.0))
    carry_ref[...] = carry
    cnt_ref[...] = carry


def _post(x2d, o_f, o_b, proj, o_da, kv3, S, w, tm):
    T = x2d.shape[0]
    per_b = S // tm
    row = lambda i: (i, 0)
    const = lambda i: (0, 0)
    return pl.pallas_call(
        functools.partial(_post_kernel, tm=tm),
        grid=(T // tm,),
        in_specs=[pl.BlockSpec((tm, D_MODEL), row),
                  pl.BlockSpec((tm, HG_WIDTH), row),
                  pl.BlockSpec((tm, HG_WIDTH), row),
                  pl.BlockSpec((tm, HG_WIDTH), lambda i: (i, CW_HGATE)),
                  pl.BlockSpec((tm, DA_WIDTH), row),
                  pl.BlockSpec((None, N_MEM, 2 * D_MODEL), lambda i: (i // per_b, 0, 0)),
                  pl.BlockSpec((1, HG_WIDTH), const),
                  pl.BlockSpec((D_MODEL, D_MODEL), const),
                  pl.BlockSpec((1, D_MODEL), const),
                  pl.BlockSpec((D_MODEL, D_MODEL), const),
                  pl.BlockSpec((D_MODEL, D_MODEL), const),
                  pl.BlockSpec((1, D_MODEL), const),
                  pl.BlockSpec((2, D_MODEL, LANES), lambda i: (0, 0, 0)),
                  pl.BlockSpec((1, LANES), const)],
        out_specs=[pl.BlockSpec((tm, D_MODEL), row),
                   pl.BlockSpec((tm, D_PACK), row),
                   pl.BlockSpec((tm, LANES), row),
                   pl.BlockSpec((tm, LANES), row),
                   pl.BlockSpec((1, LANES), const)],
        out_shape=[jax.ShapeDtypeStruct((T, D_MODEL), F32),
                   jax.ShapeDtypeStruct((T, D_PACK), jnp.uint32),
                   jax.ShapeDtypeStruct((T, LANES), jnp.int32),
                   jax.ShapeDtypeStruct((T, LANES), F32),
                   jax.ShapeDtypeStruct((1, LANES), F32)],
        scratch_shapes=[pltpu.VMEM((1, LANES), F32)],
        compiler_params=_cparams(("arbitrary",)),
        name="post_mixer",
    )(x2d, o_f, o_b, proj, o_da, kv3, w["g_hg"], w["w_out"], w["g_ca"], w["w_cq"], w["w_co"], w["g_ffn"],
      w["w_r"], w["b_r"])


GATHER_SLOTS = 3


def _expert_kernel(be_ref, nu_ref, src_ref, src_prev_ref, tok_ref, tok_next_ref, tok_next2_ref, hn_ref, wg_ref, wu_ref,
                   wd_ref, yt_ref, xbuf_ref, ybuf_ref, zbuf_ref, gsem, ssem, *, n_tok):
    del be_ref
    i = pl.program_id(0)
    n_used = nu_ref[0]
    gslot = i % GATHER_SLOTS
    yslot = i % 2

    def gather(tok, buf_slot, r):
        return pltpu.make_async_copy(hn_ref.at[pl.ds(tok[0, r], 1), :], xbuf_ref.at[buf_slot, pl.ds(r, 1), :],
                                     gsem.at[buf_slot])

    def scatter(src, buf_slot, r):
        return pltpu.make_async_copy(ybuf_ref.at[buf_slot, pl.ds(r, 1), :], yt_ref.at[pl.ds(src[0, r], 1), :],
                                     ssem.at[buf_slot])

    def zero_fill(k):
        return pltpu.make_async_copy(zbuf_ref, yt_ref.at[pl.ds(2 * n_tok + k * MOE_BLOCK, MOE_BLOCK), :], ssem.at[0])

    def for_rows(fn):
        def body(r, c):
            fn(r)
            return c
        lax.fori_loop(0, MOE_BLOCK, body, 0, unroll=8)

    @pl.when((i == 0) & (n_used > 0))
    def _():
        ybuf_ref[...] = jnp.zeros_like(ybuf_ref)
        zbuf_ref[...] = jnp.zeros_like(zbuf_ref)
        n_fill = (yt_ref.shape[0] - 2 * n_tok) // MOE_BLOCK
        for k in range(n_fill):
            zero_fill(k).start()
        for k in range(n_fill):
            zero_fill(k).wait()
        for_rows(lambda r: gather(tok_ref, 0, r).start())
        for_rows(lambda r: gather(tok_next_ref, 1, r).start())

    @pl.when(i < n_used)
    def _():
        for_rows(lambda r: gather(tok_ref, gslot, r).wait())
        for r in range(MOE_BLOCK):
            gather(tok_next2_ref, (i + 2) % GATHER_SLOTS, r).start(priority=r % 2)
            scatter(src_prev_ref, 1 - yslot, r).start(priority=(r + 1) % 2)
        xb = _unpack_bf16_pairs(xbuf_ref[gslot]).astype(BF16)
        a = jnp.dot(xb, wg_ref[...], preferred_element_type=F32)
        u = jnp.dot(xb, wu_ref[...], preferred_element_type=F32)
        hmid = (_silu(a) * u).astype(BF16)
        y = _pack_bf16_pairs(jnp.dot(hmid, wd_ref[...], preferred_element_type=F32))

        @pl.when(i > 0)
        def _():
            for_rows(lambda r: scatter(src_ref, yslot, r).wait())

        ybuf_ref[yslot] = y

        @pl.when(i == n_used - 1)
        def _():
            for_rows(lambda r: scatter(src_ref, yslot, r).start())
            for_rows(lambda r: scatter(src_ref, yslot, r).wait())
            for_rows(lambda r: scatter(src_ref, 1 - yslot, r).wait())
            for_rows(lambda r: gather(tok_ref, (i + 1) % GATHER_SLOTS, r).wait())
            for_rows(lambda r: gather(tok_ref, (i + 2) % GATHER_SLOTS, r).wait())


def _experts(hn, slot_src, block_expert, n_used, w_gate, w_up, w_down):
    n_tok = hn.shape[0]
    n_slots = slot_src.shape[0]
    nb = n_slots // MOE_BLOCK
    lead = 2 * n_tok + jnp.arange(MOE_BLOCK, dtype=jnp.int32)
    src3 = jnp.concatenate([lead, slot_src]).reshape(nb + 1, 1, MOE_BLOCK)
    tok3 = jnp.where(src3 < n_tok, src3, jnp.where(src3 < 2 * n_tok, src3 - n_tok, n_tok - 1))
    n_rows = n_slots + MOE_BLOCK

    def src_spec(off):
        return pl.BlockSpec((None, 1, MOE_BLOCK), lambda i, be, nu: (jnp.minimum(i + 1 + off, nb), 0, 0),
                            memory_space=pltpu.SMEM)

    return pl.pallas_call(
        functools.partial(_expert_kernel, n_tok=n_tok),
        grid_spec=pltpu.PrefetchScalarGridSpec(
            num_scalar_prefetch=2,
            grid=(nb,),
            in_specs=[src_spec(0), src_spec(-1), src_spec(0), src_spec(1), src_spec(2),
                      pl.BlockSpec(memory_space=pl.ANY),
                      pl.BlockSpec((None, D_MODEL, D_EXPERT), lambda i, be, nu: (be[i], 0, 0)),
                      pl.BlockSpec((None, D_MODEL, D_EXPERT), lambda i, be, nu: (be[i], 0, 0)),
                      pl.BlockSpec((None, D_EXPERT, D_MODEL), lambda i, be, nu: (be[i], 0, 0))],
            out_specs=pl.BlockSpec(memory_space=pl.ANY),
            scratch_shapes=[pltpu.VMEM((GATHER_SLOTS, MOE_BLOCK, D_PACK), jnp.uint32),
                            pltpu.VMEM((2, MOE_BLOCK, D_PACK), jnp.uint32),
                            pltpu.VMEM((MOE_BLOCK, D_PACK), jnp.uint32),
                            pltpu.SemaphoreType.DMA((GATHER_SLOTS,)), pltpu.SemaphoreType.DMA((2,))]),
        out_shape=jax.ShapeDtypeStruct((n_rows, D_PACK), jnp.uint32),
        compiler_params=_cparams(("arbitrary",), has_side_effects=True),
        name="moe_experts",
    )(block_expert, n_used, src3, src3, tok3, tok3, tok3, hn, w_gate, w_up, w_down)


def _combine_kernel(x2_ref, rw_ref, g_ref, y1_ref, y2_ref, y_ref):
    rw = rw_ref[...]
    lane = lax.broadcasted_iota(jnp.int32, rw.shape, 1)
    w1 = jnp.sum(jnp.where(lane == 0, rw, 0.0), axis=-1, keepdims=True)
    w2 = jnp.sum(jnp.where(lane == 1, rw, 0.0), axis=-1, keepdims=True)
    x3 = x2_ref[...] + (_unpack_bf16_pairs(y1_ref[...]) * w1 + _unpack_bf16_pairs(y2_ref[...]) * w2)
    y_ref[...] = _rms(x3, g_ref[...])


def _combine(x2, rw, g_final, yt, tm):
    T = x2.shape[0]
    return pl.pallas_call(
        _combine_kernel,
        grid=(T // tm,),
        in_specs=[pl.BlockSpec((tm, D_MODEL), lambda i: (i, 0)),
                  pl.BlockSpec((tm, LANES), lambda i: (i, 0)),
                  pl.BlockSpec((1, D_MODEL), lambda i: (0, 0)),
                  pl.BlockSpec((tm, D_PACK), lambda i: (i, 0)),
                  pl.BlockSpec((tm, D_PACK), lambda i: (i + T // tm, 0))],
        out_specs=pl.BlockSpec((tm, D_MODEL), lambda i: (i, 0)),
        out_shape=jax.ShapeDtypeStruct((T, D_MODEL), F32),
        compiler_params=_cparams(("parallel",)),
        name="moe_combine",
    )(x2, rw, g_final, yt, yt)


def _tile(n, pref):
    return min(n, pref)


def _trunk(x, mem, w):
    B, S, _ = x.shape
    T = B * S
    x2d = x.reshape(T, D_MODEL)
    proj, vt = _inproj(x2d, w["g_mix"], w["w_in"], w["w_vt"], _tile(T, PROJ_TM))
    proj3 = proj.reshape(B, S, IN_COLS)
    o_f, o_b = _hgrn(proj3, w["lb"], _tile(S, HG_ROWS))
    o_da = _diff_attn(proj3, vt, w["attn_scalars"], w["g_da"], _tile(S, ATT_T), w["da_out_scale"])
    kv = _memkv(mem.reshape(B * N_MEM, D_MODEL), w["g_mem"], w["w_ckv"]).reshape(B, N_MEM, 2 * D_MODEL)
    tm = _tile(S, POST_TM)
    x2, hn, ri, rw, cnt = _post(x2d, o_f.reshape(T, HG_WIDTH), o_b.reshape(T, HG_WIDTH), proj,
                                o_da.reshape(T, DA_WIDTH), kv, S, w, tm)

    counts = cnt[0, N_GROUPS:N_GROUPS + N_EXPERTS].astype(jnp.int32)
    padded = (counts + MOE_BLOCK - 1) // MOE_BLOCK * MOE_BLOCK
    pad_end = jnp.cumsum(padded)
    pad_start = pad_end - padded
    dest = pad_start[ri[:, 0:2]] + ri[:, 2:4]
    n_slots = 2 * T + N_EXPERTS * MOE_BLOCK
    nb = n_slots // MOE_BLOCK
    block_start = jnp.arange(nb, dtype=jnp.int32) * MOE_BLOCK
    block_expert = jnp.minimum(jnp.sum((pad_end[None, :] <= block_start[:, None]).astype(jnp.int32), axis=1),
                               N_EXPERTS - 1)
    n_used = (pad_end[-1:] // MOE_BLOCK).astype(jnp.int32)
    e_slot = jnp.repeat(block_expert, MOE_BLOCK)
    pad_before = jnp.cumsum(padded - counts) - (padded - counts)
    pad_rank = pad_before[e_slot] + jnp.arange(n_slots, dtype=jnp.int32) - pad_start[e_slot] - counts[e_slot]
    assign = jnp.arange(2 * T, dtype=jnp.int32)
    slot_src = (2 * T + MOE_BLOCK + pad_rank).at[dest.reshape(-1)].set((assign & 1) * T + (assign >> 1))
    yt = _experts(hn, slot_src, block_expert, n_used, w["w_gate"], w["w_up"], w["w_down"])
    y = _combine(x2, rw, w["g_final"], yt, tm)
    return y.reshape(B, S, D_MODEL)


def _split_bf16(a):
    hi = a.astype(BF16)
    return jnp.stack([hi, (a - hi.astype(F32)).astype(BF16)])


def kernel(x_prompt, x_sample, mem_prompt, mem_sample, g_mix, w_in, hg_lb, g_hg, lam_q1, lam_k1, lam_q2, lam_k2,
           g_da, w_out, g_ca, g_mem, w_cq, w_ckv, w_co, g_ffn, w_rg, b_rg, w_re, b_re, w_gate, w_up, w_down, g_final):
    l = 0
    lam_init = 0.8 - 0.6 * math.exp(-0.3 * l)
    lam = (jnp.exp(jnp.sum(lam_q1[l] * lam_k1[l])) - jnp.exp(jnp.sum(lam_q2[l] * lam_k2[l])) + lam_init)
    slopes = jnp.power(2.0, -8.0 * jnp.arange(1, DA_HEADS + 1, dtype=F32) / DA_HEADS)
    pad = LANES - N_GROUPS - N_EXPERTS
    w = {
        "g_mix": g_mix[l][None], "w_in": w_in[l].astype(BF16),
        "w_vt": w_in[l][:, CB_DV * LANES:].T.astype(BF16),
        "lb": jnp.cumsum(jax.nn.softmax(hg_lb, axis=0), axis=0)[l],
        "g_hg": g_hg[l][None],
        "attn_scalars": jnp.concatenate([slopes, lam[None]]).astype(F32),
        "g_da": g_da[l][:, None], "da_out_scale": 1.0 - lam_init,
        "w_out": w_out[l].astype(BF16), "g_ca": g_ca[l][None], "g_mem": g_mem[l][None],
        "w_cq": w_cq[l].astype(BF16), "w_ckv": w_ckv[l].astype(BF16), "w_co": w_co[l].astype(BF16),
        "g_ffn": g_ffn[l][None],
        "w_r": _split_bf16(jnp.pad(jnp.concatenate([w_rg[l], w_re[l]], axis=1), ((0, 0), (0, pad)))),
        "b_r": jnp.pad(jnp.concatenate([b_rg[l], b_re[l]]), (0, pad))[None],
        "w_gate": w_gate[l].astype(BF16), "w_up": w_up[l].astype(BF16), "w_down": w_down[l].astype(BF16),
        "g_final": g_final[None],
    }
    return (_trunk(x_prompt, mem_prompt, w), _trunk(x_sample, mem_sample, w))
```
